```python
import math
import jax, jax.numpy as jnp
from jax import lax
import numpy as np

D_MODEL = 1024
BATCH = 8
SEQ = 8192
DEPTH = 1

MIX_WIDTH = D_MODEL
HEAD_DIM = 64
ATTN_WIDTH = MIX_WIDTH // 2
CONV_WIDTH = MIX_WIDTH - ATTN_WIDTH
N_HEADS = ATTN_WIDTH // HEAD_DIM
N_KV_HEADS = 2
GROUP = N_HEADS // N_KV_HEADS
KV_WIDTH = N_KV_HEADS * HEAD_DIM
CONV_GROUPS = CONV_WIDTH // HEAD_DIM
CONV_K = 3
WINDOW = 128
BLOCK = 128
ROPE_THETA = 500000.0
ROT_DIM = HEAD_DIM // 4
D_FF = 4 * D_MODEL
NORM_EPS = 1e-6
NEG_INF = -1e30
SPLIT_SIZES = (ATTN_WIDTH, KV_WIDTH, KV_WIDTH, CONV_WIDTH, CONV_WIDTH, CONV_WIDTH)
IN_COLS = sum(SPLIT_SIZES)

kernel_name = "hymba_swa_sink_shortconv_sqrelu_sandwich"


def _rmsnorm(x, gain):
    x32 = x.astype(jnp.float32)
    y = x32 * lax.rsqrt(jnp.mean(x32 * x32, axis=-1, keepdims=True) + NORM_EPS)
    return (y * gain.astype(jnp.float32)).astype(x.dtype)


def _partial_rope(t, cos, sin):
    half = ROT_DIM // 2
    t1 = t[..., :half]
    t2 = t[..., half:ROT_DIM]
    rot = jnp.concatenate([t1 * cos - t2 * sin, t2 * cos + t1 * sin], axis=-1)
    return jnp.concatenate([rot, t[..., ROT_DIM:]], axis=-1)


def _rope_tables(seq, dtype):
    pos = jnp.arange(seq, dtype=jnp.float32)
    inv_freq = ROPE_THETA ** (-jnp.arange(0, ROT_DIM, 2, dtype=jnp.float32) / ROT_DIM)
    ang = pos[:, None] * inv_freq[None, :]
    cos = jnp.cos(ang)[None, :, None, :].astype(dtype)
    sin = jnp.sin(ang)[None, :, None, :].astype(dtype)
    return cos, sin


def _sliding_window_attention(q, k, v, sinks):
    b, s = q.shape[0], q.shape[1]
    nb = s // BLOCK
    qb = q.reshape(b, nb, BLOCK, N_KV_HEADS, GROUP, HEAD_DIM)

    def band(t):
        tp = jnp.pad(t, ((0, 0), (BLOCK, 0), (0, 0), (0, 0)))
        tp = tp.reshape(b, nb + 1, BLOCK, N_KV_HEADS, HEAD_DIM)
        return jnp.concatenate([tp[:, :-1], tp[:, 1:]], axis=2)

    kb, vb = band(k), band(v)
    scale = 1.0 / math.sqrt(HEAD_DIM)
    scores = jnp.einsum('bnqkgd,bnskd->bnkgqs', qb, kb).astype(jnp.float32) * scale

    qi = jnp.arange(BLOCK)[:, None]
    kj = jnp.arange(2 * BLOCK)[None, :]
    dist = BLOCK + qi - kj
    in_win = (dist >= 0) & (dist < WINDOW)
    key_abs = jnp.arange(nb)[:, None, None] * BLOCK - BLOCK + kj[None]
    valid = in_win[None] & (key_abs >= 0)
    scores = jnp.where(valid[None, :, None, None], scores, NEG_INF)

    sink = jnp.broadcast_to(sinks.astype(jnp.float32).reshape(1, 1, N_KV_HEADS, GROUP, 1, 1),
                            scores.shape[:-1] + (1,))
    probs = jax.nn.softmax(jnp.concatenate([scores, sink], axis=-1), axis=-1)[..., :-1]
    out = jnp.einsum('bnkgqs,bnskd->bnqkgd', probs.astype(v.dtype), vb)
    return out.reshape(b, s, N_HEADS * HEAD_DIM)


def _short_conv(u, w):
    s = u.shape[1]
    up = jnp.pad(u, ((0, 0), (CONV_K - 1, 0), (0, 0)))
    return sum(up[:, j:j + s, :] * w[j] for j in range(CONV_K))


def setup_inputs(seed: int = 0) -> dict:
    key = jax.random.key(seed)
    ks = jax.random.split(key, 16)
    f32 = jnp.float32

    def gain(k, n):
        return 1.0 + 0.02 * jax.random.normal(k, (DEPTH, n), f32)

    return {
        "x": jax.random.normal(ks[0], (BATCH, SEQ, D_MODEL), f32),
        "pre_mix_norm": gain(ks[1], D_MODEL),
        "w_in": jax.random.normal(ks[2], (DEPTH, D_MODEL, IN_COLS), f32) * D_MODEL ** -0.5,
        "conv_w": jax.random.normal(ks[3], (DEPTH, CONV_K, CONV_WIDTH), f32) * CONV_K ** -0.5,
        "attn_sinks": 0.5 * jax.random.normal(ks[4], (DEPTH, N_HEADS), f32),
        "attn_group_norm": gain(ks[5], ATTN_WIDTH),
        "conv_group_norm": gain(ks[6], CONV_WIDTH),
        "w_out": jax.random.normal(ks[7], (DEPTH, MIX_WIDTH, D_MODEL), f32) * MIX_WIDTH ** -0.5,
        "post_mix_norm": gain(ks[8], D_MODEL),
        "pre_mlp_norm": gain(ks[9], D_MODEL),
        "w_up": jax.random.normal(ks[10], (DEPTH, D_MODEL, D_FF), f32) * D_MODEL ** -0.5,
        "w_down": jax.random.normal(ks[11], (DEPTH, D_FF, D_MODEL), f32) * D_FF ** -0.5,
        "post_mlp_norm": gain(ks[12], D_MODEL),
    }


def reference(x, pre_mix_norm, w_in, conv_w, attn_sinks, attn_group_norm, conv_group_norm,
              w_out, post_mix_norm, pre_mlp_norm, w_up, w_down, post_mlp_norm):
    b, s, _ = x.shape
    cos, sin = _rope_tables(s, x.dtype)
    split_idx = list(np.cumsum(SPLIT_SIZES)[:-1])
    h = x
    for l in range(DEPTH):
        hn = _rmsnorm(h, pre_mix_norm[l])
        proj = jnp.einsum('bsd,de->bse', hn, w_in[l])
        q, k, v, gb, gc, xin = jnp.split(proj, split_idx, axis=-1)

        q = _partial_rope(q.reshape(b, s, N_HEADS, HEAD_DIM), cos, sin)
        k = _partial_rope(k.reshape(b, s, N_KV_HEADS, HEAD_DIM), cos, sin)
        v = v.reshape(b, s, N_KV_HEADS, HEAD_DIM)
        attn = _sliding_window_attention(q, k, v, attn_sinks[l])

        conv = gb * _short_conv(gc * xin, conv_w[l])

        mixed = jnp.concatenate([_rmsnorm(attn, attn_group_norm[l]),
                                 _rmsnorm(conv, conv_group_norm[l])], axis=-1)
        mix_out = jnp.einsum('bse,ed->bsd', mixed, w_out[l])
        h = h + _rmsnorm(mix_out, post_mix_norm[l])

        hn = _rmsnorm(h, pre_mlp_norm[l])
        up = jax.nn.relu(jnp.einsum('bsd,df->bsf', hn, w_up[l]))
        mlp_out = jnp.einsum('bsf,fd->bsd', up * up, w_down[l])
        h = h + _rmsnorm(mlp_out, post_mlp_norm[l])
    return h
```

```python
import functools
import math

import jax
import jax.numpy as jnp
from jax import lax
from jax.experimental import pallas as pl
from jax.experimental.pallas import tpu as pltpu

D_MODEL = 1024
HEAD_DIM = 64
ATTN_WIDTH = 512
CONV_WIDTH = 512
N_HEADS = 8
N_KV_HEADS = 2
GROUP = N_HEADS // N_KV_HEADS
KV_WIDTH = N_KV_HEADS * HEAD_DIM
CONV_K = 3
BLOCK = 128
ROPE_THETA = 500000.0
ROT_DIM = HEAD_DIM // 4
D_FF = 4 * D_MODEL
NORM_EPS = 1e-6
NEG_INF = -1e30
IN_COLS = ATTN_WIDTH + 2 * KV_WIDTH + 3 * CONV_WIDTH

LANES = 128
GROUP_WIDTH = GROUP * HEAD_DIM

Q_OFF = 0
K_OFF = ATTN_WIDTH
V_OFF = K_OFF + KV_WIDTH
B_OFF = V_OFF + KV_WIDTH
C_OFF = B_OFF + CONV_WIDTH
X_OFF = C_OFF + CONV_WIDTH

MIX_TILE = 512
MLP_TILE = 512
FF_CHUNK = 1024
CONV_PAD = 8
VMEM_LIMIT_BYTES = 56 * 1024 * 1024


def _rms(x, gain):
    ms = jnp.mean(x * x, axis=-1, keepdims=True)
    return x * lax.rsqrt(ms + NORM_EPS) * gain


def _rope(t, cosf, sinf, lo_half):
    fwd = pltpu.roll(t, LANES - ROT_DIM // 2, axis=1)
    bwd = pltpu.roll(t, ROT_DIM // 2, axis=1)
    return t * cosf + jnp.where(lo_half, fwd, bwd) * sinf


def _mixer_kernel(x_ref, cos_ref, sin_ref, g_pre_ref, w_in_ref, conv_w_ref, sinks_ref,
                  g_attn_ref, g_conv_ref, w_out_ref, g_post_ref, o_ref,
                  kbuf, vbuf, ubuf):
    T = x_ref.shape[0]
    nblk = T // BLOCK
    s_idx = pl.program_id(1)

    @pl.when(s_idx == 0)
    def _():
        kbuf[0:BLOCK, :] = jnp.zeros((BLOCK, kbuf.shape[1]), kbuf.dtype)
        vbuf[0:BLOCK, :] = jnp.zeros((BLOCK, vbuf.shape[1]), vbuf.dtype)
        ubuf[0:CONV_PAD, :] = jnp.zeros((CONV_PAD, CONV_WIDTH), ubuf.dtype)

    x = x_ref[...]
    hn = _rms(x, g_pre_ref[...]).astype(jnp.bfloat16)
    proj = jnp.dot(hn, w_in_ref[...], preferred_element_type=jnp.float32)

    lane = lax.broadcasted_iota(jnp.int32, (T, LANES), 1)
    in_head = lane % HEAD_DIM
    lo_half = in_head < ROT_DIM // 2
    first_head = lane < HEAD_DIM
    cosf = cos_ref[...]
    sinf = sin_ref[...]

    scale = 1.0 / math.sqrt(HEAD_DIM)
    q_cols = []
    for c in range(ATTN_WIDTH // LANES):
        qc = proj[:, Q_OFF + c * LANES:Q_OFF + (c + 1) * LANES]
        q_cols.append((_rope(qc, cosf, sinf, lo_half) * scale).astype(jnp.bfloat16))

    k = _rope(proj[:, K_OFF:K_OFF + KV_WIDTH], cosf, sinf, lo_half)
    k_sw = pltpu.roll(k, HEAD_DIM, axis=1)
    kbuf[BLOCK:BLOCK + T, 0:LANES] = jnp.where(first_head, k, k_sw).astype(jnp.bfloat16)
    kbuf[BLOCK:BLOCK + T, LANES:2 * LANES] = jnp.where(first_head, k_sw, k).astype(jnp.bfloat16)

    v = proj[:, V_OFF:V_OFF + KV_WIDTH]
    v_sw = pltpu.roll(v, HEAD_DIM, axis=1)
    zero = jnp.zeros_like(v)
    vbuf[BLOCK:BLOCK + T, 0 * LANES:1 * LANES] = jnp.where(first_head, v, zero).astype(jnp.bfloat16)
    vbuf[BLOCK:BLOCK + T, 1 * LANES:2 * LANES] = jnp.where(first_head, zero, v_sw).astype(jnp.bfloat16)
    vbuf[BLOCK:BLOCK + T, 2 * LANES:3 * LANES] = jnp.where(first_head, v_sw, zero).astype(jnp.bfloat16)
    vbuf[BLOCK:BLOCK + T, 3 * LANES:4 * LANES] = jnp.where(first_head, zero, v).astype(jnp.bfloat16)

    row = lax.broadcasted_iota(jnp.int32, (BLOCK, 2 * BLOCK), 0)
    col = lax.broadcasted_iota(jnp.int32, (BLOCK, 2 * BLOCK), 1)
    valid_inner = (col > row) & (col <= row + BLOCK)
    first_thr = jnp.where(s_idx == 0, BLOCK - 1, -1)
    valid_first = (col > jnp.maximum(row, first_thr)) & (col <= row + BLOCK)

    glane = lax.broadcasted_iota(jnp.int32, (BLOCK, GROUP_WIDTH), 1)
    head_masks = [(glane >= hh * HEAD_DIM) & (glane < (hh + 1) * HEAD_DIM) for hh in range(GROUP)]
    zslab = jnp.zeros((2 * BLOCK, LANES), jnp.bfloat16)

    attn_rows = []
    for i in range(nblk):
        valid = valid_first if i == 0 else valid_inner
        r0 = i * BLOCK
        group_out = []
        for g in range(N_KV_HEADS):
            qg = jnp.concatenate(
                [q_cols[2 * g][r0:r0 + BLOCK], q_cols[2 * g + 1][r0:r0 + BLOCK]], axis=1)
            kd = kbuf[r0:r0 + 2 * BLOCK, g * LANES:(g + 1) * LANES]
            krep = jnp.concatenate([kd, kd], axis=1)
            v_lo = vbuf[r0:r0 + 2 * BLOCK, (2 * g) * LANES:(2 * g + 1) * LANES]
            v_hi = vbuf[r0:r0 + 2 * BLOCK, (2 * g + 1) * LANES:(2 * g + 2) * LANES]
            v_heads = [jnp.concatenate([v_lo, zslab], axis=1),
                       jnp.concatenate([v_hi, zslab], axis=1),
                       jnp.concatenate([zslab, v_lo], axis=1),
                       jnp.concatenate([zslab, v_hi], axis=1)]
            acc = None
            for hh in range(GROUP):
                qm = jnp.where(head_masks[hh], qg, jnp.zeros_like(qg))
                s = lax.dot_general(qm, krep, (((1,), (1,)), ((), ())),
                                    preferred_element_type=jnp.float32)
                s = jnp.where(valid, s, NEG_INF)
                sink = sinks_ref[g * GROUP + hh]
                m = jnp.maximum(jnp.max(s, axis=-1, keepdims=True), sink)
                e = jnp.exp(s - m)
                denom = jnp.sum(e, axis=-1, keepdims=True) + jnp.exp(sink - m)
                p = (e / denom).astype(jnp.bfloat16)
                pv = jnp.dot(p, v_heads[hh], preferred_element_type=jnp.float32)
                acc = pv if acc is None else acc + pv
            group_out.append(acc)
        attn_rows.append(jnp.concatenate(group_out, axis=1))
    attn = jnp.concatenate(attn_rows, axis=0)

    kbuf[0:BLOCK, :] = kbuf[T:T + BLOCK, :]
    vbuf[0:BLOCK, :] = vbuf[T:T + BLOCK, :]

    u = proj[:, C_OFF:C_OFF + CONV_WIDTH] * proj[:, X_OFF:X_OFF + CONV_WIDTH]
    ubuf[CONV_PAD:CONV_PAD + T, :] = u
    cw = conv_w_ref[...]
    conv = (ubuf[CONV_PAD - 2:CONV_PAD - 2 + T, :] * cw[0:1, :]
            + ubuf[CONV_PAD - 1:CONV_PAD - 1 + T, :] * cw[1:2, :]
            + u * cw[2:3, :])
    conv = proj[:, B_OFF:B_OFF + CONV_WIDTH] * conv
    ubuf[0:CONV_PAD, :] = ubuf[T:T + CONV_PAD, :]

    mixed = jnp.concatenate(
        [_rms(attn, g_attn_ref[...]).astype(jnp.bfloat16),
         _rms(conv, g_conv_ref[...]).astype(jnp.bfloat16)], axis=1)
    mix_out = jnp.dot(mixed, w_out_ref[...], preferred_element_type=jnp.float32)
    o_ref[...] = x + _rms(mix_out, g_post_ref[...])


def _mlp_kernel(h_ref, g_pre_ref, w_up_ref, w_down_ref, g_post_ref, o_ref):
    h = h_ref[...]
    hn = _rms(h, g_pre_ref[...]).astype(jnp.bfloat16)
    acc = None
    for c in range(D_FF // FF_CHUNK):
        up = jnp.dot(hn, w_up_ref[:, c * FF_CHUNK:(c + 1) * FF_CHUNK],
                     preferred_element_type=jnp.float32)
        up = jnp.maximum(up, 0.0)
        act = (up * up).astype(jnp.bfloat16)
        part = jnp.dot(act, w_down_ref[c * FF_CHUNK:(c + 1) * FF_CHUNK, :],
                       preferred_element_type=jnp.float32)
        acc = part if acc is None else acc + part
    o_ref[...] = h + _rms(acc, g_post_ref[...])


def _rope_lane_tables(seq):
    pos = jnp.arange(seq, dtype=jnp.float32)
    inv_freq = ROPE_THETA ** (-jnp.arange(0, ROT_DIM, 2, dtype=jnp.float32) / ROT_DIM)
    ang = pos[:, None] * inv_freq[None, :]
    cos = jnp.cos(ang)
    sin = jnp.sin(ang)
    pad = HEAD_DIM - ROT_DIM
    cos_head = jnp.concatenate([cos, cos, jnp.ones((seq, pad), jnp.float32)], axis=1)
    sin_head = jnp.concatenate([-sin, sin, jnp.zeros((seq, pad), jnp.float32)], axis=1)
    reps = LANES // HEAD_DIM
    return jnp.tile(cos_head, (1, reps)), jnp.tile(sin_head, (1, reps))


def _const_spec(shape):
    return pl.BlockSpec(shape, lambda *_: (0,) * len(shape))


@jax.jit
def kernel(x, pre_mix_norm, w_in, conv_w, attn_sinks, attn_group_norm, conv_group_norm,
           w_out, post_mix_norm, pre_mlp_norm, w_up, w_down, post_mlp_norm):
    b, s, d = x.shape
    assert d == D_MODEL and s % MIX_TILE == 0 and (b * s) % MLP_TILE == 0
    assert pre_mix_norm.shape[0] == 1, "single-layer kernel"
    cosf, sinf = _rope_lane_tables(s)
    bf = jnp.bfloat16

    mixer = pl.pallas_call(
        _mixer_kernel,
        name="mixer",
        grid=(b, s // MIX_TILE),
        in_specs=[
            pl.BlockSpec((None, MIX_TILE, D_MODEL), lambda bi, si: (bi, si, 0)),
            pl.BlockSpec((MIX_TILE, LANES), lambda bi, si: (si, 0)),
            pl.BlockSpec((MIX_TILE, LANES), lambda bi, si: (si, 0)),
            _const_spec((1, D_MODEL)),
            _const_spec((D_MODEL, IN_COLS)),
            _const_spec((CONV_K, CONV_WIDTH)),
            pl.BlockSpec(memory_space=pltpu.SMEM),
            _const_spec((1, ATTN_WIDTH)),
            _const_spec((1, CONV_WIDTH)),
            _const_spec((D_MODEL, D_MODEL)),
            _const_spec((1, D_MODEL)),
        ],
        out_specs=pl.BlockSpec((None, MIX_TILE, D_MODEL), lambda bi, si: (bi, si, 0)),
        out_shape=jax.ShapeDtypeStruct((b, s, d), x.dtype),
        scratch_shapes=[
            pltpu.VMEM((MIX_TILE + BLOCK, 2 * LANES), bf),
            pltpu.VMEM((MIX_TILE + BLOCK, 4 * LANES), bf),
            pltpu.VMEM((MIX_TILE + CONV_PAD, CONV_WIDTH), jnp.float32),
        ],
        compiler_params=pltpu.CompilerParams(
            dimension_semantics=("arbitrary", "arbitrary"),
            vmem_limit_bytes=VMEM_LIMIT_BYTES),
    )
    h = mixer(x, cosf, sinf, pre_mix_norm, w_in[0].astype(bf), conv_w[0], attn_sinks[0],
              attn_group_norm, conv_group_norm, w_out[0].astype(bf), post_mix_norm)

    n_tok = b * s
    mlp = pl.pallas_call(
        _mlp_kernel,
        name="mlp",
        grid=(n_tok // MLP_TILE,),
        in_specs=[
            pl.BlockSpec((MLP_TILE, D_MODEL), lambda i: (i, 0)),
            _const_spec((1, D_MODEL)),
            _const_spec((D_MODEL, D_FF)),
            _const_spec((D_FF, D_MODEL)),
            _const_spec((1, D_MODEL)),
        ],
        out_specs=pl.BlockSpec((MLP_TILE, D_MODEL), lambda i: (i, 0)),
        out_shape=jax.ShapeDtypeStruct((n_tok, d), x.dtype),
        compiler_params=pltpu.CompilerParams(
            dimension_semantics=("arbitrary",),
            vmem_limit_bytes=VMEM_LIMIT_BYTES),
    )
    out = mlp(h.reshape(n_tok, d), pre_mlp_norm, w_up[0].astype(bf), w_down[0].astype(bf),
              post_mlp_norm)
    return out.reshape(b, s, d)
```

```python
import functools
import math

import jax
import jax.numpy as jnp
from jax import lax
from jax.experimental import pallas as pl
from jax.experimental.pallas import tpu as pltpu

D_MODEL = 1024
HEAD_DIM = 64
ATTN_WIDTH = 512
CONV_WIDTH = 512
N_HEADS = 8
N_KV_HEADS = 2
GROUP = N_HEADS // N_KV_HEADS
KV_WIDTH = N_KV_HEADS * HEAD_DIM
CONV_K = 3
BLOCK = 128
ROPE_THETA = 500000.0
ROT_DIM = HEAD_DIM // 4
D_FF = 4 * D_MODEL
NORM_EPS = 1e-6
NEG_INF = -1e30
IN_COLS = ATTN_WIDTH + 2 * KV_WIDTH + 3 * CONV_WIDTH

LANES = 128
GROUP_WIDTH = GROUP * HEAD_DIM

Q_OFF = 0
K_OFF = ATTN_WIDTH
V_OFF = K_OFF + KV_WIDTH
B_OFF = V_OFF + KV_WIDTH
C_OFF = B_OFF + CONV_WIDTH
X_OFF = C_OFF + CONV_WIDTH

MIX_TILE = 512
MLP_TILE = 512
FF_CHUNK = 1024
CONV_PAD = 8
VMEM_LIMIT_BYTES = 56 * 1024 * 1024


def _rms(x, gain):
    ms = jnp.mean(x * x, axis=-1, keepdims=True)
    return x * lax.rsqrt(ms + NORM_EPS) * gain


def _rope(t, cosf, sinf, lo_half):
    fwd = pltpu.roll(t, LANES - ROT_DIM // 2, axis=1)
    bwd = pltpu.roll(t, ROT_DIM // 2, axis=1)
    return t * cosf + jnp.where(lo_half, fwd, bwd) * sinf


def _mixer_kernel(x_ref, cos_ref, sin_ref, g_pre_ref, w_in_ref, conv_w_ref, sinks_ref,
                  g_attn_ref, g_conv_ref, w_out_ref, g_post_ref, o_ref,
                  kbuf, vbuf, ubuf):
    T = x_ref.shape[0]
    nblk = T // BLOCK
    s_idx = pl.program_id(1)

    @pl.when(s_idx == 0)
    def _():
        kbuf[0:BLOCK, :] = jnp.zeros((BLOCK, kbuf.shape[1]), kbuf.dtype)
        vbuf[0:BLOCK, :] = jnp.zeros((BLOCK, vbuf.shape[1]), vbuf.dtype)
        ubuf[0:CONV_PAD, :] = jnp.zeros((CONV_PAD, CONV_WIDTH), ubuf.dtype)

    x = x_ref[...]
    hn = _rms(x, g_pre_ref[...]).astype(jnp.bfloat16)
    proj = jnp.dot(hn, w_in_ref[...], preferred_element_type=jnp.float32)

    lane = lax.broadcasted_iota(jnp.int32, (T, LANES), 1)
    in_head = lane % HEAD_DIM
    lo_half = in_head < ROT_DIM // 2
    first_head = lane < HEAD_DIM
    cosf = cos_ref[...]
    sinf = sin_ref[...]

    scale = 1.0 / math.sqrt(HEAD_DIM)
    q_cols = []
    for c in range(ATTN_WIDTH // LANES):
        qc = proj[:, Q_OFF + c * LANES:Q_OFF + (c + 1) * LANES]
        q_cols.append((_rope(qc, cosf, sinf, lo_half) * scale).astype(jnp.bfloat16))

    k = _rope(proj[:, K_OFF:K_OFF + KV_WIDTH], cosf, sinf, lo_half)
    k_sw = pltpu.roll(k, HEAD_DIM, axis=1)
    kbuf[BLOCK:BLOCK + T, 0:LANES] = jnp.where(first_head, k, k_sw).astype(jnp.bfloat16)
    kbuf[BLOCK:BLOCK + T, LANES:2 * LANES] = jnp.where(first_head, k_sw, k).astype(jnp.bfloat16)

    v = proj[:, V_OFF:V_OFF + KV_WIDTH]
    v_sw = pltpu.roll(v, HEAD_DIM, axis=1)
    vbuf[BLOCK:BLOCK + T, 0:LANES] = jnp.where(first_head, v, v_sw).astype(jnp.bfloat16)
    vbuf[BLOCK:BLOCK + T, LANES:2 * LANES] = jnp.where(first_head, v_sw, v).astype(jnp.bfloat16)

    GM = GROUP * BLOCK
    srow = lax.broadcasted_iota(jnp.int32, (GM, 2 * BLOCK), 0) % BLOCK
    scol = lax.broadcasted_iota(jnp.int32, (GM, 2 * BLOCK), 1)
    valid_inner = (scol > srow) & (scol <= srow + BLOCK)
    first_thr = jnp.where(s_idx == 0, BLOCK - 1, -1)
    valid_first = (scol > jnp.maximum(srow, first_thr)) & (scol <= srow + BLOCK)

    grow = lax.broadcasted_iota(jnp.int32, (GM, GROUP_WIDTH), 0) // BLOCK
    glane = lax.broadcasted_iota(jnp.int32, (GM, GROUP_WIDTH), 1) // HEAD_DIM
    own_lanes = grow == glane
    out_head = lax.broadcasted_iota(jnp.int32, (BLOCK, GROUP_WIDTH), 1) // HEAD_DIM
    hrow = lax.broadcasted_iota(jnp.int32, (GM, 1), 0) // BLOCK
    sink_cols = []
    for g in range(N_KV_HEADS):
        sc = jnp.full((GM, 1), sinks_ref[g * GROUP], jnp.float32)
        for hh in range(1, GROUP):
            sc = jnp.where(hrow == hh, sinks_ref[g * GROUP + hh], sc)
        sink_cols.append(sc)

    attn_rows = []
    for i in range(nblk):
        valid = valid_first if i == 0 else valid_inner
        r0 = i * BLOCK
        group_out = []
        for g in range(N_KV_HEADS):
            qg = jnp.concatenate(
                [q_cols[2 * g][r0:r0 + BLOCK], q_cols[2 * g + 1][r0:r0 + BLOCK]], axis=1)
            qs = jnp.concatenate([qg] * GROUP, axis=0)
            qs = jnp.where(own_lanes, qs, jnp.zeros_like(qs))
            kd = kbuf[r0:r0 + 2 * BLOCK, g * LANES:(g + 1) * LANES]
            krep = jnp.concatenate([kd, kd], axis=1)
            vd = vbuf[r0:r0 + 2 * BLOCK, g * LANES:(g + 1) * LANES]
            vrep = jnp.concatenate([vd, vd], axis=1)
            s = lax.dot_general(qs, krep, (((1,), (1,)), ((), ())),
                                preferred_element_type=jnp.float32)
            s = jnp.where(valid, s, NEG_INF)
            sink = sink_cols[g]
            m = jnp.maximum(jnp.max(s, axis=-1, keepdims=True), sink)
            e = jnp.exp(s - m)
            denom = jnp.sum(e, axis=-1, keepdims=True) + jnp.exp(sink - m)
            p = (e * (1.0 / denom)).astype(jnp.bfloat16)
            pv = jnp.dot(p, vrep, preferred_element_type=jnp.float32)
            out = pv[(GROUP - 1) * BLOCK:GROUP * BLOCK]
            for hh in range(GROUP - 2, -1, -1):
                out = jnp.where(out_head == hh, pv[hh * BLOCK:(hh + 1) * BLOCK], out)
            group_out.append(out)
        attn_rows.append(jnp.concatenate(group_out, axis=1))
    attn = jnp.concatenate(attn_rows, axis=0)

    kbuf[0:BLOCK, :] = kbuf[T:T + BLOCK, :]
    vbuf[0:BLOCK, :] = vbuf[T:T + BLOCK, :]

    u = proj[:, C_OFF:C_OFF + CONV_WIDTH] * proj[:, X_OFF:X_OFF + CONV_WIDTH]
    ubuf[CONV_PAD:CONV_PAD + T, :] = u
    cw = conv_w_ref[...]
    conv = (ubuf[CONV_PAD - 2:CONV_PAD - 2 + T, :] * cw[0:1, :]
            + ubuf[CONV_PAD - 1:CONV_PAD - 1 + T, :] * cw[1:2, :]
            + u * cw[2:3, :])
    conv = proj[:, B_OFF:B_OFF + CONV_WIDTH] * conv
    ubuf[0:CONV_PAD, :] = ubuf[T:T + CONV_PAD, :]

    mixed = jnp.concatenate(
        [_rms(attn, g_attn_ref[...]).astype(jnp.bfloat16),
         _rms(conv, g_conv_ref[...]).astype(jnp.bfloat16)], axis=1)
    mix_out = jnp.dot(mixed, w_out_ref[...], preferred_element_type=jnp.float32)
    o_ref[...] = x + _rms(mix_out, g_post_ref[...])


def _mlp_kernel(h_ref, g_pre_ref, w_up_ref, w_down_ref, g_post_ref, o_ref):
    h = h_ref[...]
    hn = _rms(h, g_pre_ref[...]).astype(jnp.bfloat16)
    acc = None
    for c in range(D_FF // FF_CHUNK):
        up = jnp.dot(hn, w_up_ref[:, c * FF_CHUNK:(c + 1) * FF_CHUNK],
                     preferred_element_type=jnp.float32)
        up = jnp.maximum(up, 0.0)
        act = (up * up).astype(jnp.bfloat16)
        part = jnp.dot(act, w_down_ref[c * FF_CHUNK:(c + 1) * FF_CHUNK, :],
                       preferred_element_type=jnp.float32)
        acc = part if acc is None else acc + part
    o_ref[...] = h + _rms(acc, g_post_ref[...])


def _rope_lane_tables(seq):
    pos = jnp.arange(seq, dtype=jnp.float32)
    inv_freq = ROPE_THETA ** (-jnp.arange(0, ROT_DIM, 2, dtype=jnp.float32) / ROT_DIM)
    ang = pos[:, None] * inv_freq[None, :]
    cos = jnp.cos(ang)
    sin = jnp.sin(ang)
    pad = HEAD_DIM - ROT_DIM
    cos_head = jnp.concatenate([cos, cos, jnp.ones((seq, pad), jnp.float32)], axis=1)
    sin_head = jnp.concatenate([-sin, sin, jnp.zeros((seq, pad), jnp.float32)], axis=1)
    reps = LANES // HEAD_DIM
    return jnp.tile(cos_head, (1, reps)), jnp.tile(sin_head, (1, reps))


def _const_spec(shape):
    return pl.BlockSpec(shape, lambda *_: (0,) * len(shape))


@jax.jit
def kernel(x, pre_mix_norm, w_in, conv_w, attn_sinks, attn_group_norm, conv_group_norm,
           w_out, post_mix_norm, pre_mlp_norm, w_up, w_down, post_mlp_norm):
    b, s, d = x.shape
    assert d == D_MODEL and s % MIX_TILE == 0 and (b * s) % MLP_TILE == 0
    assert pre_mix_norm.shape[0] == 1, "single-layer kernel"
    cosf, sinf = _rope_lane_tables(s)
    bf = jnp.bfloat16

    mixer = pl.pallas_call(
        _mixer_kernel,
        name="mixer",
        grid=(b, s // MIX_TILE),
        in_specs=[
            pl.BlockSpec((None, MIX_TILE, D_MODEL), lambda bi, si: (bi, si, 0)),
            pl.BlockSpec((MIX_TILE, LANES), lambda bi, si: (si, 0)),
            pl.BlockSpec((MIX_TILE, LANES), lambda bi, si: (si, 0)),
            _const_spec((1, D_MODEL)),
            _const_spec((D_MODEL, IN_COLS)),
            _const_spec((CONV_K, CONV_WIDTH)),
            pl.BlockSpec(memory_space=pltpu.SMEM),
            _const_spec((1, ATTN_WIDTH)),
            _const_spec((1, CONV_WIDTH)),
            _const_spec((D_MODEL, D_MODEL)),
            _const_spec((1, D_MODEL)),
        ],
        out_specs=pl.BlockSpec((None, MIX_TILE, D_MODEL), lambda bi, si: (bi, si, 0)),
        out_shape=jax.ShapeDtypeStruct((b, s, d), x.dtype),
        scratch_shapes=[
            pltpu.VMEM((MIX_TILE + BLOCK, 2 * LANES), bf),
            pltpu.VMEM((MIX_TILE + BLOCK, 2 * LANES), bf),
            pltpu.VMEM((MIX_TILE + CONV_PAD, CONV_WIDTH), jnp.float32),
        ],
        compiler_params=pltpu.CompilerParams(
            dimension_semantics=("arbitrary", "arbitrary"),
            vmem_limit_bytes=VMEM_LIMIT_BYTES),
    )
    h = mixer(x, cosf, sinf, pre_mix_norm, w_in[0].astype(bf), conv_w[0], attn_sinks[0],
              attn_group_norm, conv_group_norm, w_out[0].astype(bf), post_mix_norm)

    n_tok = b * s
    mlp = pl.pallas_call(
        _mlp_kernel,
        name="mlp",
        grid=(n_tok // MLP_TILE,),
        in_specs=[
            pl.BlockSpec((MLP_TILE, D_MODEL), lambda i: (i, 0)),
            _const_spec((1, D_MODEL)),
            _const_spec((D_MODEL, D_FF)),
            _const_spec((D_FF, D_MODEL)),
            _const_spec((1, D_MODEL)),
        ],
        out_specs=pl.BlockSpec((MLP_TILE, D_MODEL), lambda i: (i, 0)),
        out_shape=jax.ShapeDtypeStruct((n_tok, d), x.dtype),
        compiler_params=pltpu.CompilerParams(
            dimension_semantics=("arbitrary",),
            vmem_limit_bytes=VMEM_LIMIT_BYTES),
    )
    out = mlp(h.reshape(n_tok, d), pre_mlp_norm, w_up[0].astype(bf), w_down[0].astype(bf),
              post_mlp_norm)
    return out.reshape(b, s, d)
```

```python
import math

import jax
import jax.numpy as jnp
from jax import lax
from jax.experimental import pallas as pl
from jax.experimental.pallas import tpu as pltpu

D_MODEL = 1024
HEAD_DIM = 64
ATTN_WIDTH = 512
CONV_WIDTH = 512
N_HEADS = 8
N_KV_HEADS = 2
GROUP = N_HEADS // N_KV_HEADS
KV_WIDTH = N_KV_HEADS * HEAD_DIM
CONV_K = 3
BLOCK = 128
ROPE_THETA = 500000.0
ROT_DIM = HEAD_DIM // 4
D_FF = 4 * D_MODEL
NORM_EPS = 1e-6
NEG_INF = -1e30
IN_COLS = ATTN_WIDTH + 2 * KV_WIDTH + 3 * CONV_WIDTH

LANES = 128
GROUP_WIDTH = GROUP * HEAD_DIM
GROUP_ROWS = GROUP * BLOCK

Q_OFF = 0
K_OFF = ATTN_WIDTH
V_OFF = K_OFF + KV_WIDTH
B_OFF = V_OFF + KV_WIDTH

MIX_TILE = 512
MLP_TILE = 512
FF_CHUNK = 1024
CONV_PAD = 8
VMEM_LIMIT_BYTES = 56 * 1024 * 1024


def _rms(x, gain):
    ms = jnp.mean(x * x, axis=-1, keepdims=True)
    return x * lax.rsqrt(ms + NORM_EPS) * gain


def _rope(t, cosf, sinf, lo_half):
    fwd = pltpu.roll(t, LANES - ROT_DIM // 2, axis=1)
    bwd = pltpu.roll(t, ROT_DIM // 2, axis=1)
    return t * cosf + jnp.where(lo_half, fwd, bwd) * sinf


def _mixer_kernel(x_ref, cos_ref, sin_ref, g_pre_ref, w_in_ref, conv_w_ref, sinks_ref,
                  g_attn_ref, g_conv_ref, w_out_ref, g_post_ref, o_ref,
                  kbuf, vbuf, ubuf):
    T = x_ref.shape[0]
    nblk = T // BLOCK
    s_idx = pl.program_id(1)

    @pl.when(s_idx == 0)
    def _():
        kbuf[0:BLOCK, :] = jnp.zeros((BLOCK, kbuf.shape[1]), kbuf.dtype)
        vbuf[0:BLOCK, :] = jnp.zeros((BLOCK, vbuf.shape[1]), vbuf.dtype)
        ubuf[0:CONV_PAD, :] = jnp.zeros((CONV_PAD, CONV_WIDTH), ubuf.dtype)

    x = x_ref[...]
    hn = _rms(x, g_pre_ref[...]).astype(jnp.bfloat16)
    proj = jnp.dot(hn, w_in_ref[:, 0:B_OFF], preferred_element_type=jnp.float32)
    cproj = jnp.dot(hn, w_in_ref[:, B_OFF:IN_COLS], preferred_element_type=jnp.float32)

    lane = lax.broadcasted_iota(jnp.int32, (T, LANES), 1)
    lo_half = lane % HEAD_DIM < ROT_DIM // 2
    first_head = lane < HEAD_DIM
    cosf = cos_ref[...]
    sinf = sin_ref[...]

    scale = 1.0 / math.sqrt(HEAD_DIM)
    q_cols = []
    for c in range(ATTN_WIDTH // LANES):
        qc = proj[:, Q_OFF + c * LANES:Q_OFF + (c + 1) * LANES]
        q_cols.append((_rope(qc, cosf, sinf, lo_half) * scale).astype(jnp.bfloat16))

    k = _rope(proj[:, K_OFF:K_OFF + KV_WIDTH], cosf, sinf, lo_half)
    k_sw = pltpu.roll(k, HEAD_DIM, axis=1)
    kbuf[BLOCK:BLOCK + T, 0:LANES] = jnp.where(first_head, k, k_sw).astype(jnp.bfloat16)
    kbuf[BLOCK:BLOCK + T, LANES:2 * LANES] = jnp.where(first_head, k_sw, k).astype(jnp.bfloat16)

    v = proj[:, V_OFF:V_OFF + KV_WIDTH]
    v_sw = pltpu.roll(v, HEAD_DIM, axis=1)
    vbuf[BLOCK:BLOCK + T, 0:LANES] = jnp.where(first_head, v, v_sw).astype(jnp.bfloat16)
    vbuf[BLOCK:BLOCK + T, LANES:2 * LANES] = jnp.where(first_head, v_sw, v).astype(jnp.bfloat16)

    u = cproj[:, CONV_WIDTH:2 * CONV_WIDTH] * cproj[:, 2 * CONV_WIDTH:3 * CONV_WIDTH]
    ubuf[CONV_PAD:CONV_PAD + T, :] = u
    cw = conv_w_ref[...]
    conv = (ubuf[CONV_PAD - 2:CONV_PAD - 2 + T, :] * cw[0:1, :]
            + ubuf[CONV_PAD - 1:CONV_PAD - 1 + T, :] * cw[1:2, :]
            + u * cw[2:3, :])
    conv = cproj[:, 0:CONV_WIDTH] * conv
    ubuf[0:CONV_PAD, :] = ubuf[T:T + CONV_PAD, :]
    conv_n = _rms(conv, g_conv_ref[...]).astype(jnp.bfloat16)
    mix_conv = jnp.dot(conv_n, w_out_ref[ATTN_WIDTH:, :], preferred_element_type=jnp.float32)

    srow = lax.broadcasted_iota(jnp.int32, (GROUP_ROWS, BLOCK), 0) % BLOCK
    scol = lax.broadcasted_iota(jnp.int32, (GROUP_ROWS, BLOCK), 1)
    own_side = scol <= srow
    first_pen = jnp.where(s_idx == 0, NEG_INF, 0.0)

    grow = lax.broadcasted_iota(jnp.int32, (GROUP_ROWS, GROUP_WIDTH), 0) // BLOCK
    glane = lax.broadcasted_iota(jnp.int32, (GROUP_ROWS, GROUP_WIDTH), 1) // HEAD_DIM
    own_lanes = grow == glane
    out_head = lax.broadcasted_iota(jnp.int32, (BLOCK, GROUP_WIDTH), 1) // HEAD_DIM
    hrow = lax.broadcasted_iota(jnp.int32, (GROUP_ROWS, 1), 0) // BLOCK
    sink_cols = []
    for g in range(N_KV_HEADS):
        sc = jnp.full((GROUP_ROWS, 1), sinks_ref[g * GROUP], jnp.float32)
        for hh in range(1, GROUP):
            sc = jnp.where(hrow == hh, sinks_ref[g * GROUP + hh], sc)
        sink_cols.append(sc)

    def scores(i, g):
        r0 = i * BLOCK
        qg = jnp.concatenate(
            [q_cols[2 * g][r0:r0 + BLOCK], q_cols[2 * g + 1][r0:r0 + BLOCK]], axis=1)
        qs = jnp.concatenate([qg] * GROUP, axis=0)
        qs = jnp.where(own_lanes, qs, jnp.zeros_like(qs))
        kd = kbuf[r0:r0 + 2 * BLOCK, g * LANES:(g + 1) * LANES]
        krep = jnp.concatenate([kd, kd], axis=1)
        return lax.dot_general(qs, krep, (((1,), (1,)), ((), ())),
                               preferred_element_type=jnp.float32)

    def attend(i, g, s):
        r0 = i * BLOCK
        s_prev = s[:, 0:BLOCK]
        if i == 0:
            s_prev = s_prev + first_pen
        s = jnp.where(own_side, s[:, BLOCK:2 * BLOCK], s_prev)
        sink = sink_cols[g]
        m = jnp.max(s, axis=-1, keepdims=True)
        e = jnp.exp(s - m)
        denom = jnp.sum(e, axis=-1, keepdims=True) + jnp.exp(sink - m)
        p = (e * (1.0 / denom)).astype(jnp.bfloat16)
        zeros = jnp.zeros_like(p)
        p = jnp.concatenate([jnp.where(own_side, zeros, p), jnp.where(own_side, p, zeros)], axis=1)
        vd = vbuf[r0:r0 + 2 * BLOCK, g * LANES:(g + 1) * LANES]
        vrep = jnp.concatenate([vd, vd], axis=1)
        pv = jnp.dot(p, vrep, preferred_element_type=jnp.float32)
        out = pv[(GROUP - 1) * BLOCK:GROUP * BLOCK]
        for hh in range(GROUP - 2, -1, -1):
            out = jnp.where(out_head == hh, pv[hh * BLOCK:(hh + 1) * BLOCK], out)
        return out

    items = [(i, g) for i in range(nblk) for g in range(N_KV_HEADS)]
    outs = {}
    s_next = scores(*items[0])
    for n, (i, g) in enumerate(items):
        s_cur = s_next
        if n + 1 < len(items):
            s_next = scores(*items[n + 1])
        outs[(i, g)] = attend(i, g, s_cur)
    attn = jnp.concatenate(
        [jnp.concatenate([outs[(i, g)] for g in range(N_KV_HEADS)], axis=1) for i in range(nblk)],
        axis=0)

    kbuf[0:BLOCK, :] = kbuf[T:T + BLOCK, :]
    vbuf[0:BLOCK, :] = vbuf[T:T + BLOCK, :]

    attn_n = _rms(attn, g_attn_ref[...]).astype(jnp.bfloat16)
    mix_out = mix_conv + jnp.dot(attn_n, w_out_ref[0:ATTN_WIDTH, :],
                                 preferred_element_type=jnp.float32)
    o_ref[...] = x + _rms(mix_out, g_post_ref[...])


def _mlp_kernel(h_ref, g_pre_ref, w_up_ref, w_down_ref, g_post_ref, o_ref):
    h = h_ref[...]
    hn = _rms(h, g_pre_ref[...]).astype(jnp.bfloat16)
    acc = None
    for c in range(D_FF // FF_CHUNK):
        up = jnp.dot(hn, w_up_ref[:, c * FF_CHUNK:(c + 1) * FF_CHUNK],
                     preferred_element_type=jnp.float32)
        up = jnp.maximum(up, 0.0)
        act = (up * up).astype(jnp.bfloat16)
        part = jnp.dot(act, w_down_ref[c * FF_CHUNK:(c + 1) * FF_CHUNK, :],
                       preferred_element_type=jnp.float32)
        acc = part if acc is None else acc + part
    o_ref[...] = h + _rms(acc, g_post_ref[...])


def _rope_lane_tables(seq):
    pos = jnp.arange(seq, dtype=jnp.float32)
    inv_freq = ROPE_THETA ** (-jnp.arange(0, ROT_DIM, 2, dtype=jnp.float32) / ROT_DIM)
    ang = pos[:, None] * inv_freq[None, :]
    cos = jnp.cos(ang)
    sin = jnp.sin(ang)
    pad = HEAD_DIM - ROT_DIM
    cos_head = jnp.concatenate([cos, cos, jnp.ones((seq, pad), jnp.float32)], axis=1)
    sin_head = jnp.concatenate([-sin, sin, jnp.zeros((seq, pad), jnp.float32)], axis=1)
    reps = LANES // HEAD_DIM
    return jnp.tile(cos_head, (1, reps)), jnp.tile(sin_head, (1, reps))


def _const_spec(shape):
    return pl.BlockSpec(shape, lambda *_: (0,) * len(shape))


@jax.jit
def kernel(x, pre_mix_norm, w_in, conv_w, attn_sinks, attn_group_norm, conv_group_norm,
           w_out, post_mix_norm, pre_mlp_norm, w_up, w_down, post_mlp_norm):
    b, s, d = x.shape
    assert d == D_MODEL and s % MIX_TILE == 0 and (b * s) % MLP_TILE == 0
    assert pre_mix_norm.shape[0] == 1, "single-layer kernel"
    cosf, sinf = _rope_lane_tables(s)
    bf = jnp.bfloat16

    mixer = pl.pallas_call(
        _mixer_kernel,
        name="mixer",
        grid=(b, s // MIX_TILE),
        in_specs=[
            pl.BlockSpec((None, MIX_TILE, D_MODEL), lambda bi, si: (bi, si, 0)),
            pl.BlockSpec((MIX_TILE, LANES), lambda bi, si: (si, 0)),
            pl.BlockSpec((MIX_TILE, LANES), lambda bi, si: (si, 0)),
            _const_spec((1, D_MODEL)),
            _const_spec((D_MODEL, IN_COLS)),
            _const_spec((CONV_K, CONV_WIDTH)),
            pl.BlockSpec(memory_space=pltpu.SMEM),
            _const_spec((1, ATTN_WIDTH)),
            _const_spec((1, CONV_WIDTH)),
            _const_spec((D_MODEL, D_MODEL)),
            _const_spec((1, D_MODEL)),
        ],
        out_specs=pl.BlockSpec((None, MIX_TILE, D_MODEL), lambda bi, si: (bi, si, 0)),
        out_shape=jax.ShapeDtypeStruct((b, s, d), x.dtype),
        scratch_shapes=[
            pltpu.VMEM((MIX_TILE + BLOCK, 2 * LANES), bf),
            pltpu.VMEM((MIX_TILE + BLOCK, 2 * LANES), bf),
            pltpu.VMEM((MIX_TILE + CONV_PAD, CONV_WIDTH), jnp.float32),
        ],
        compiler_params=pltpu.CompilerParams(
            dimension_semantics=("arbitrary", "arbitrary"),
            vmem_limit_bytes=VMEM_LIMIT_BYTES),
    )
    h = mixer(x, cosf, sinf, pre_mix_norm, w_in[0].astype(bf), conv_w[0], attn_sinks[0],
              attn_group_norm, conv_group_norm, w_out[0].astype(bf), post_mix_norm)

    n_tok = b * s
    mlp = pl.pallas_call(
        _mlp_kernel,
        name="mlp",
        grid=(n_tok // MLP_TILE,),
        in_specs=[
            pl.BlockSpec((MLP_TILE, D_MODEL), lambda i: (i, 0)),
            _const_spec((1, D_MODEL)),
            _const_spec((D_MODEL, D_FF)),
            _const_spec((D_FF, D_MODEL)),
            _const_spec((1, D_MODEL)),
        ],
        out_specs=pl.BlockSpec((MLP_TILE, D_MODEL), lambda i: (i, 0)),
        out_shape=jax.ShapeDtypeStruct((n_tok, d), x.dtype),
        compiler_params=pltpu.CompilerParams(
            dimension_semantics=("arbitrary",),
            vmem_limit_bytes=VMEM_LIMIT_BYTES),
    )
    out = mlp(h.reshape(n_tok, d), pre_mlp_norm, w_up[0].astype(bf), w_down[0].astype(bf),
              post_mlp_norm)
    return out.reshape(b, s, d)
```

```python
import math

import jax
import jax.numpy as jnp
from jax import lax
from jax.experimental import pallas as pl
from jax.experimental.pallas import tpu as pltpu

D_MODEL = 1024
HEAD_DIM = 64
ATTN_WIDTH = 512
CONV_WIDTH = 512
N_HEADS = 8
N_KV_HEADS = 2
GROUP = N_HEADS // N_KV_HEADS
KV_WIDTH = N_KV_HEADS * HEAD_DIM
CONV_K = 3
BLOCK = 128
ROPE_THETA = 500000.0
ROT_DIM = HEAD_DIM // 4
D_FF = 4 * D_MODEL
NORM_EPS = 1e-6
NEG_INF = -1e30
IN_COLS = ATTN_WIDTH + 2 * KV_WIDTH + 3 * CONV_WIDTH

LANES = 128
GROUP_WIDTH = GROUP * HEAD_DIM
GROUP_ROWS = GROUP * BLOCK

Q_OFF = 0
K_OFF = ATTN_WIDTH
V_OFF = K_OFF + KV_WIDTH
B_OFF = V_OFF + KV_WIDTH

MIX_TILE = 512
MLP_TILE = 1024
FF_CHUNK = 1024
CONV_PAD = 8
VMEM_LIMIT_BYTES = 56 * 1024 * 1024


def _rms(x, gain):
    ms = jnp.mean(x * x, axis=-1, keepdims=True)
    return x * lax.rsqrt(ms + NORM_EPS) * gain


def _rope(t, cosf, sinf, lo_half):
    fwd = pltpu.roll(t, LANES - ROT_DIM // 2, axis=1)
    bwd = pltpu.roll(t, ROT_DIM // 2, axis=1)
    return t * cosf + jnp.where(lo_half, fwd, bwd) * sinf


def _mixer_kernel(x_ref, cos_ref, sin_ref, g_pre_ref, w_in_ref, conv_w_ref, sinks_ref,
                  g_attn_ref, g_conv_ref, w_out_ref, g_post_ref, o_ref,
                  kbuf, vbuf, ubuf):
    T = x_ref.shape[0]
    nblk = T // BLOCK
    s_idx = pl.program_id(1)

    @pl.when(s_idx == 0)
    def _():
        kbuf[0:BLOCK, :] = jnp.zeros((BLOCK, kbuf.shape[1]), kbuf.dtype)
        vbuf[0:BLOCK, :] = jnp.zeros((BLOCK, vbuf.shape[1]), vbuf.dtype)
        ubuf[0:CONV_PAD, :] = jnp.zeros((CONV_PAD, CONV_WIDTH), ubuf.dtype)

    x = x_ref[...]
    hn = _rms(x, g_pre_ref[...]).astype(jnp.bfloat16)
    proj = jnp.dot(hn, w_in_ref[:, 0:B_OFF], preferred_element_type=jnp.float32)
    cproj = jnp.dot(hn, w_in_ref[:, B_OFF:IN_COLS], preferred_element_type=jnp.float32)

    lane = lax.broadcasted_iota(jnp.int32, (T, LANES), 1)
    lo_half = lane % HEAD_DIM < ROT_DIM // 2
    first_head = lane < HEAD_DIM
    cosf = cos_ref[...]
    sinf = sin_ref[...]

    scale = 1.0 / math.sqrt(HEAD_DIM)
    q_cols = []
    for c in range(ATTN_WIDTH // LANES):
        qc = proj[:, Q_OFF + c * LANES:Q_OFF + (c + 1) * LANES]
        q_cols.append((_rope(qc, cosf, sinf, lo_half) * scale).astype(jnp.bfloat16))

    k = _rope(proj[:, K_OFF:K_OFF + KV_WIDTH], cosf, sinf, lo_half)
    k_sw = pltpu.roll(k, HEAD_DIM, axis=1)
    kbuf[BLOCK:BLOCK + T, 0:LANES] = jnp.where(first_head, k, k_sw).astype(jnp.bfloat16)
    kbuf[BLOCK:BLOCK + T, LANES:2 * LANES] = jnp.where(first_head, k_sw, k).astype(jnp.bfloat16)

    v = proj[:, V_OFF:V_OFF + KV_WIDTH]
    v_sw = pltpu.roll(v, HEAD_DIM, axis=1)
    vbuf[BLOCK:BLOCK + T, 0:LANES] = jnp.where(first_head, v, v_sw).astype(jnp.bfloat16)
    vbuf[BLOCK:BLOCK + T, LANES:2 * LANES] = jnp.where(first_head, v_sw, v).astype(jnp.bfloat16)

    u = cproj[:, CONV_WIDTH:2 * CONV_WIDTH] * cproj[:, 2 * CONV_WIDTH:3 * CONV_WIDTH]
    ubuf[CONV_PAD:CONV_PAD + T, :] = u
    cw = conv_w_ref[...]
    conv = (ubuf[CONV_PAD - 2:CONV_PAD - 2 + T, :] * cw[0:1, :]
            + ubuf[CONV_PAD - 1:CONV_PAD - 1 + T, :] * cw[1:2, :]
            + u * cw[2:3, :])
    conv = cproj[:, 0:CONV_WIDTH] * conv
    ubuf[0:CONV_PAD, :] = ubuf[T:T + CONV_PAD, :]
    conv_n = _rms(conv, g_conv_ref[...]).astype(jnp.bfloat16)
    mix_conv = jnp.dot(conv_n, w_out_ref[ATTN_WIDTH:, :], preferred_element_type=jnp.float32)

    srow = lax.broadcasted_iota(jnp.int32, (GROUP_ROWS, BLOCK), 0) % BLOCK
    scol = lax.broadcasted_iota(jnp.int32, (GROUP_ROWS, BLOCK), 1)
    own_side = scol <= srow
    first_pen = jnp.where(s_idx == 0, NEG_INF, 0.0)

    grow = lax.broadcasted_iota(jnp.int32, (GROUP_ROWS, GROUP_WIDTH), 0) // BLOCK
    glane = lax.broadcasted_iota(jnp.int32, (GROUP_ROWS, GROUP_WIDTH), 1) // HEAD_DIM
    own_lanes = grow == glane
    out_head = lax.broadcasted_iota(jnp.int32, (BLOCK, GROUP_WIDTH), 1) // HEAD_DIM
    hrow = lax.broadcasted_iota(jnp.int32, (GROUP_ROWS, 1), 0) // BLOCK
    sink_cols = []
    for g in range(N_KV_HEADS):
        sc = jnp.full((GROUP_ROWS, 1), sinks_ref[g * GROUP], jnp.float32)
        for hh in range(1, GROUP):
            sc = jnp.where(hrow == hh, sinks_ref[g * GROUP + hh], sc)
        sink_cols.append(sc)

    def scores(i, g):
        r0 = i * BLOCK
        qg = jnp.concatenate(
            [q_cols[2 * g][r0:r0 + BLOCK], q_cols[2 * g + 1][r0:r0 + BLOCK]], axis=1)
        qs = jnp.concatenate([qg] * GROUP, axis=0)
        qs = jnp.where(own_lanes, qs, jnp.zeros_like(qs))
        kd = kbuf[r0:r0 + 2 * BLOCK, g * LANES:(g + 1) * LANES]
        krep = jnp.concatenate([kd, kd], axis=1)
        return lax.dot_general(qs, krep, (((1,), (1,)), ((), ())),
                               preferred_element_type=jnp.float32)

    def probs(i, g, s):
        s_prev = s[:, 0:BLOCK]
        if i == 0:
            s_prev = s_prev + first_pen
        s = jnp.where(own_side, s[:, BLOCK:2 * BLOCK], s_prev)
        sink = sink_cols[g]
        m = jnp.max(s, axis=-1, keepdims=True)
        e = jnp.exp(s - m)
        denom = jnp.sum(e, axis=-1, keepdims=True) + jnp.exp(sink - m)
        p = (e * (1.0 / denom)).astype(jnp.bfloat16)
        zeros = jnp.zeros_like(p)
        return jnp.concatenate([jnp.where(own_side, zeros, p), jnp.where(own_side, p, zeros)],
                               axis=1)

    def values(i, g, p):
        r0 = i * BLOCK
        vd = vbuf[r0:r0 + 2 * BLOCK, g * LANES:(g + 1) * LANES]
        vrep = jnp.concatenate([vd, vd], axis=1)
        pv = jnp.dot(p, vrep, preferred_element_type=jnp.float32)
        out = pv[(GROUP - 1) * BLOCK:GROUP * BLOCK]
        for hh in range(GROUP - 2, -1, -1):
            out = jnp.where(out_head == hh, pv[hh * BLOCK:(hh + 1) * BLOCK], out)
        return out

    items = [(i, g) for i in range(nblk) for g in range(N_KV_HEADS)]
    n_items = len(items)
    s_vals, p_vals, outs = {}, {}, {}
    for step in range(n_items + 2):
        if step < n_items:
            s_vals[step] = scores(*items[step])
        if 0 <= step - 1 < n_items:
            p_vals[step - 1] = probs(*items[step - 1], s_vals.pop(step - 1))
        if 0 <= step - 2 < n_items:
            outs[items[step - 2]] = values(*items[step - 2], p_vals.pop(step - 2))
    attn = jnp.concatenate(
        [jnp.concatenate([outs[(i, g)] for g in range(N_KV_HEADS)], axis=1) for i in range(nblk)],
        axis=0)

    kbuf[0:BLOCK, :] = kbuf[T:T + BLOCK, :]
    vbuf[0:BLOCK, :] = vbuf[T:T + BLOCK, :]

    attn_n = _rms(attn, g_attn_ref[...]).astype(jnp.bfloat16)
    mix_out = mix_conv + jnp.dot(attn_n, w_out_ref[0:ATTN_WIDTH, :],
                                 preferred_element_type=jnp.float32)
    o_ref[...] = x + _rms(mix_out, g_post_ref[...])


def _mlp_kernel(h_ref, g_pre_ref, w_up_ref, w_down_ref, g_post_ref, o_ref):
    h = h_ref[...]
    hn = _rms(h, g_pre_ref[...]).astype(jnp.bfloat16)
    acc = None
    for c in range(D_FF // FF_CHUNK):
        up = jnp.dot(hn, w_up_ref[:, c * FF_CHUNK:(c + 1) * FF_CHUNK],
                     preferred_element_type=jnp.float32)
        up = jnp.maximum(up, 0.0)
        act = (up * up).astype(jnp.bfloat16)
        part = jnp.dot(act, w_down_ref[c * FF_CHUNK:(c + 1) * FF_CHUNK, :],
                       preferred_element_type=jnp.float32)
        acc = part if acc is None else acc + part
    o_ref[...] = h + _rms(acc, g_post_ref[...])


def _rope_lane_tables(seq):
    pos = jnp.arange(seq, dtype=jnp.float32)
    inv_freq = ROPE_THETA ** (-jnp.arange(0, ROT_DIM, 2, dtype=jnp.float32) / ROT_DIM)
    ang = pos[:, None] * inv_freq[None, :]
    cos = jnp.cos(ang)
    sin = jnp.sin(ang)
    pad = HEAD_DIM - ROT_DIM
    cos_head = jnp.concatenate([cos, cos, jnp.ones((seq, pad), jnp.float32)], axis=1)
    sin_head = jnp.concatenate([-sin, sin, jnp.zeros((seq, pad), jnp.float32)], axis=1)
    reps = LANES // HEAD_DIM
    return jnp.tile(cos_head, (1, reps)), jnp.tile(sin_head, (1, reps))


def _const_spec(shape):
    return pl.BlockSpec(shape, lambda *_: (0,) * len(shape))


@jax.jit
def kernel(x, pre_mix_norm, w_in, conv_w, attn_sinks, attn_group_norm, conv_group_norm,
           w_out, post_mix_norm, pre_mlp_norm, w_up, w_down, post_mlp_norm):
    b, s, d = x.shape
    assert d == D_MODEL and s % MIX_TILE == 0 and (b * s) % MLP_TILE == 0
    assert pre_mix_norm.shape[0] == 1, "single-layer kernel"
    cosf, sinf = _rope_lane_tables(s)
    bf = jnp.bfloat16

    mixer = pl.pallas_call(
        _mixer_kernel,
        name="mixer",
        grid=(b, s // MIX_TILE),
        in_specs=[
            pl.BlockSpec((None, MIX_TILE, D_MODEL), lambda bi, si: (bi, si, 0)),
            pl.BlockSpec((MIX_TILE, LANES), lambda bi, si: (si, 0)),
            pl.BlockSpec((MIX_TILE, LANES), lambda bi, si: (si, 0)),
            _const_spec((1, D_MODEL)),
            _const_spec((D_MODEL, IN_COLS)),
            _const_spec((CONV_K, CONV_WIDTH)),
            pl.BlockSpec(memory_space=pltpu.SMEM),
            _const_spec((1, ATTN_WIDTH)),
            _const_spec((1, CONV_WIDTH)),
            _const_spec((D_MODEL, D_MODEL)),
            _const_spec((1, D_MODEL)),
        ],
        out_specs=pl.BlockSpec((None, MIX_TILE, D_MODEL), lambda bi, si: (bi, si, 0)),
        out_shape=jax.ShapeDtypeStruct((b, s, d), x.dtype),
        scratch_shapes=[
            pltpu.VMEM((MIX_TILE + BLOCK, 2 * LANES), bf),
            pltpu.VMEM((MIX_TILE + BLOCK, 2 * LANES), bf),
            pltpu.VMEM((MIX_TILE + CONV_PAD, CONV_WIDTH), jnp.float32),
        ],
        compiler_params=pltpu.CompilerParams(
            dimension_semantics=("arbitrary", "arbitrary"),
            vmem_limit_bytes=VMEM_LIMIT_BYTES),
    )
    h = mixer(x, cosf, sinf, pre_mix_norm, w_in[0].astype(bf), conv_w[0], attn_sinks[0],
              attn_group_norm, conv_group_norm, w_out[0].astype(bf), post_mix_norm)

    n_tok = b * s
    mlp = pl.pallas_call(
        _mlp_kernel,
        name="mlp",
        grid=(n_tok // MLP_TILE,),
        in_specs=[
            pl.BlockSpec((MLP_TILE, D_MODEL), lambda i: (i, 0)),
            _const_spec((1, D_MODEL)),
            _const_spec((D_MODEL, D_FF)),
            _const_spec((D_FF, D_MODEL)),
            _const_spec((1, D_MODEL)),
        ],
        out_specs=pl.BlockSpec((MLP_TILE, D_MODEL), lambda i: (i, 0)),
        out_shape=jax.ShapeDtypeStruct((n_tok, d), x.dtype),
        compiler_params=pltpu.CompilerParams(
            dimension_semantics=("arbitrary",),
            vmem_limit_bytes=VMEM_LIMIT_BYTES),
    )
    out = mlp(h.reshape(n_tok, d), pre_mlp_norm, w_up[0].astype(bf), w_down[0].astype(bf),
              post_mlp_norm)
    return out.reshape(b, s, d)
```

```python
import math

import jax
import jax.numpy as jnp
from jax import lax
from jax.experimental import pallas as pl
from jax.experimental.pallas import tpu as pltpu

D_MODEL = 1024
HEAD_DIM = 64
ATTN_WIDTH = 512
CONV_WIDTH = 512
N_HEADS = 8
N_KV_HEADS = 2
GROUP = N_HEADS // N_KV_HEADS
KV_WIDTH = N_KV_HEADS * HEAD_DIM
CONV_K = 3
BLOCK = 128
ROPE_THETA = 500000.0
ROT_DIM = HEAD_DIM // 4
D_FF = 4 * D_MODEL
NORM_EPS = 1e-6
NEG_INF = -1e30
IN_COLS = ATTN_WIDTH + 2 * KV_WIDTH + 3 * CONV_WIDTH

LANES = 128
GROUP_WIDTH = GROUP * HEAD_DIM
GROUP_ROWS = GROUP * BLOCK

Q_OFF = 0
K_OFF = ATTN_WIDTH
V_OFF = K_OFF + KV_WIDTH
B_OFF = V_OFF + KV_WIDTH

MIX_TILE = 512
MLP_TILE = 1024
FF_CHUNK = 1024
CONV_PAD = 8
VMEM_LIMIT_BYTES = 56 * 1024 * 1024


def _rms(x, gain):
    ms = jnp.mean(x * x, axis=-1, keepdims=True)
    return x * lax.rsqrt(ms + NORM_EPS) * gain


def _rope(t, cosf, sinf, lo_half):
    fwd = pltpu.roll(t, LANES - ROT_DIM // 2, axis=1)
    bwd = pltpu.roll(t, ROT_DIM // 2, axis=1)
    return t * cosf + jnp.where(lo_half, fwd, bwd) * sinf


def _mixer_kernel(x_ref, cos_ref, sin_ref, g_pre_ref, w_in_ref, conv_w_ref, sinks_ref,
                  g_attn_ref, g_conv_ref, w_out_ref, g_post_ref, o_ref,
                  kbuf, vbuf, ubuf):
    T = x_ref.shape[0]
    nblk = T // BLOCK
    s_idx = pl.program_id(1)

    @pl.when(s_idx == 0)
    def _():
        kbuf[0:BLOCK, :] = jnp.zeros((BLOCK, kbuf.shape[1]), kbuf.dtype)
        vbuf[0:BLOCK, :] = jnp.zeros((BLOCK, vbuf.shape[1]), vbuf.dtype)
        ubuf[0:CONV_PAD, :] = jnp.zeros((CONV_PAD, CONV_WIDTH), ubuf.dtype)

    x = x_ref[...]
    hn = _rms(x, g_pre_ref[...]).astype(jnp.bfloat16)
    proj = jnp.dot(hn, w_in_ref[:, 0:B_OFF], preferred_element_type=jnp.float32)
    cproj = jnp.dot(hn, w_in_ref[:, B_OFF:IN_COLS], preferred_element_type=jnp.float32)

    lane = lax.broadcasted_iota(jnp.int32, (T, LANES), 1)
    lo_half = lane % HEAD_DIM < ROT_DIM // 2
    first_head = lane < HEAD_DIM
    cosf = cos_ref[...]
    sinf = sin_ref[...]

    scale = 1.0 / math.sqrt(HEAD_DIM)
    q_cols = []
    for c in range(ATTN_WIDTH // LANES):
        qc = proj[:, Q_OFF + c * LANES:Q_OFF + (c + 1) * LANES]
        q_cols.append((_rope(qc, cosf, sinf, lo_half) * scale).astype(jnp.bfloat16))

    k = _rope(proj[:, K_OFF:K_OFF + KV_WIDTH], cosf, sinf, lo_half)
    k_sw = pltpu.roll(k, HEAD_DIM, axis=1)
    kbuf[BLOCK:BLOCK + T, 0:LANES] = jnp.where(first_head, k, k_sw).astype(jnp.bfloat16)
    kbuf[BLOCK:BLOCK + T, LANES:2 * LANES] = jnp.where(first_head, k_sw, k).astype(jnp.bfloat16)

    v = proj[:, V_OFF:V_OFF + KV_WIDTH]
    v_sw = pltpu.roll(v, HEAD_DIM, axis=1)
    vbuf[BLOCK:BLOCK + T, 0:LANES] = jnp.where(first_head, v, v_sw).astype(jnp.bfloat16)
    vbuf[BLOCK:BLOCK + T, LANES:2 * LANES] = jnp.where(first_head, v_sw, v).astype(jnp.bfloat16)

    u = cproj[:, CONV_WIDTH:2 * CONV_WIDTH] * cproj[:, 2 * CONV_WIDTH:3 * CONV_WIDTH]
    ubuf[CONV_PAD:CONV_PAD + T, :] = u
    cw = conv_w_ref[...]
    conv = (ubuf[CONV_PAD - 2:CONV_PAD - 2 + T, :] * cw[0:1, :]
            + ubuf[CONV_PAD - 1:CONV_PAD - 1 + T, :] * cw[1:2, :]
            + u * cw[2:3, :])
    conv = cproj[:, 0:CONV_WIDTH] * conv
    ubuf[0:CONV_PAD, :] = ubuf[T:T + CONV_PAD, :]
    conv_n = _rms(conv, g_conv_ref[...]).astype(jnp.bfloat16)

    srow = lax.broadcasted_iota(jnp.int32, (GROUP_ROWS, BLOCK), 0) % BLOCK
    scol = lax.broadcasted_iota(jnp.int32, (GROUP_ROWS, BLOCK), 1)
    own_side = scol <= srow
    first_pen = jnp.where(s_idx == 0, NEG_INF, 0.0)

    grow = lax.broadcasted_iota(jnp.int32, (GROUP_ROWS, GROUP_WIDTH), 0) // BLOCK
    glane = lax.broadcasted_iota(jnp.int32, (GROUP_ROWS, GROUP_WIDTH), 1) // HEAD_DIM
    own_lanes = grow == glane
    out_head = lax.broadcasted_iota(jnp.int32, (BLOCK, GROUP_WIDTH), 1) // HEAD_DIM
    hrow = lax.broadcasted_iota(jnp.int32, (GROUP_ROWS, 1), 0) // BLOCK
    sink_cols = []
    for g in range(N_KV_HEADS):
        sc = jnp.full((GROUP_ROWS, 1), sinks_ref[g * GROUP], jnp.float32)
        for hh in range(1, GROUP):
            sc = jnp.where(hrow == hh, sinks_ref[g * GROUP + hh], sc)
        sink_cols.append(sc)

    def scores(i, g):
        r0 = i * BLOCK
        qg = jnp.concatenate(
            [q_cols[2 * g][r0:r0 + BLOCK], q_cols[2 * g + 1][r0:r0 + BLOCK]], axis=1)
        qs = jnp.concatenate([qg] * GROUP, axis=0)
        qs = jnp.where(own_lanes, qs, jnp.zeros_like(qs))
        kd = kbuf[r0:r0 + 2 * BLOCK, g * LANES:(g + 1) * LANES]
        krep = jnp.concatenate([kd, kd], axis=1)
        return lax.dot_general(qs, krep, (((1,), (1,)), ((), ())),
                               preferred_element_type=jnp.float32)

    def probs(i, g, s):
        s_prev = s[:, 0:BLOCK]
        if i == 0:
            s_prev = s_prev + first_pen
        s = jnp.where(own_side, s[:, BLOCK:2 * BLOCK], s_prev)
        sink = sink_cols[g]
        m = jnp.max(s, axis=-1, keepdims=True)
        e = jnp.exp(s - m)
        denom = jnp.sum(e, axis=-1, keepdims=True) + jnp.exp(sink - m)
        p = (e * (1.0 / denom)).astype(jnp.bfloat16)
        zeros = jnp.zeros_like(p)
        return jnp.concatenate([jnp.where(own_side, zeros, p), jnp.where(own_side, p, zeros)],
                               axis=1)

    def values(i, g, p):
        r0 = i * BLOCK
        vd = vbuf[r0:r0 + 2 * BLOCK, g * LANES:(g + 1) * LANES]
        vrep = jnp.concatenate([vd, vd], axis=1)
        pv = jnp.dot(p, vrep, preferred_element_type=jnp.float32)
        out = pv[(GROUP - 1) * BLOCK:GROUP * BLOCK]
        for hh in range(GROUP - 2, -1, -1):
            out = jnp.where(out_head == hh, pv[hh * BLOCK:(hh + 1) * BLOCK], out)
        return out

    items = [(i, g) for i in range(nblk) for g in range(N_KV_HEADS)]
    n_items = len(items)
    s_vals, p_vals, outs = {}, {}, {}
    mix_conv_cols = []
    half = D_MODEL // 2
    for step in range(n_items + 2):
        if step < n_items:
            s_vals[step] = scores(*items[step])
        else:
            c0 = (step - n_items) * half
            mix_conv_cols.append(jnp.dot(conv_n, w_out_ref[ATTN_WIDTH:, c0:c0 + half],
                                         preferred_element_type=jnp.float32))
        if 0 <= step - 1 < n_items:
            p_vals[step - 1] = probs(*items[step - 1], s_vals.pop(step - 1))
        if 0 <= step - 2 < n_items:
            outs[items[step - 2]] = values(*items[step - 2], p_vals.pop(step - 2))
    attn = jnp.concatenate(
        [jnp.concatenate([outs[(i, g)] for g in range(N_KV_HEADS)], axis=1) for i in range(nblk)],
        axis=0)

    kbuf[0:BLOCK, :] = kbuf[T:T + BLOCK, :]
    vbuf[0:BLOCK, :] = vbuf[T:T + BLOCK, :]

    mix_conv = jnp.where(s_idx >= 0, jnp.concatenate(mix_conv_cols, axis=1), 0.0)
    attn_n = _rms(attn, g_attn_ref[...]).astype(jnp.bfloat16)
    mix_out = mix_conv + jnp.dot(attn_n, w_out_ref[0:ATTN_WIDTH, :],
                                 preferred_element_type=jnp.float32)
    o_ref[...] = x + _rms(mix_out, g_post_ref[...])


def _mlp_kernel(h_ref, g_pre_ref, w_up_ref, w_down_ref, g_post_ref, o_ref):
    h = h_ref[...]
    hn = _rms(h, g_pre_ref[...]).astype(jnp.bfloat16)
    acc = None
    for c in range(D_FF // FF_CHUNK):
        up = jnp.dot(hn, w_up_ref[:, c * FF_CHUNK:(c + 1) * FF_CHUNK],
                     preferred_element_type=jnp.float32)
        up = jnp.maximum(up, 0.0)
        act = (up * up).astype(jnp.bfloat16)
        part = jnp.dot(act, w_down_ref[c * FF_CHUNK:(c + 1) * FF_CHUNK, :],
                       preferred_element_type=jnp.float32)
        acc = part if acc is None else acc + part
    o_ref[...] = h + _rms(acc, g_post_ref[...])


def _rope_lane_tables(seq):
    pos = jnp.arange(seq, dtype=jnp.float32)
    inv_freq = ROPE_THETA ** (-jnp.arange(0, ROT_DIM, 2, dtype=jnp.float32) / ROT_DIM)
    ang = pos[:, None] * inv_freq[None, :]
    cos = jnp.cos(ang)
    sin = jnp.sin(ang)
    pad = HEAD_DIM - ROT_DIM
    cos_head = jnp.concatenate([cos, cos, jnp.ones((seq, pad), jnp.float32)], axis=1)
    sin_head = jnp.concatenate([-sin, sin, jnp.zeros((seq, pad), jnp.float32)], axis=1)
    reps = LANES // HEAD_DIM
    return jnp.tile(cos_head, (1, reps)), jnp.tile(sin_head, (1, reps))


def _const_spec(shape):
    return pl.BlockSpec(shape, lambda *_: (0,) * len(shape))


@jax.jit
def kernel(x, pre_mix_norm, w_in, conv_w, attn_sinks, attn_group_norm, conv_group_norm,
           w_out, post_mix_norm, pre_mlp_norm, w_up, w_down, post_mlp_norm):
    b, s, d = x.shape
    assert d == D_MODEL and s % MIX_TILE == 0 and (b * s) % MLP_TILE == 0
    assert pre_mix_norm.shape[0] == 1, "single-layer kernel"
    cosf, sinf = _rope_lane_tables(s)
    bf = jnp.bfloat16

    mixer = pl.pallas_call(
        _mixer_kernel,
        name="mixer",
        grid=(b, s // MIX_TILE),
        in_specs=[
            pl.BlockSpec((None, MIX_TILE, D_MODEL), lambda bi, si: (bi, si, 0)),
            pl.BlockSpec((MIX_TILE, LANES), lambda bi, si: (si, 0)),
            pl.BlockSpec((MIX_TILE, LANES), lambda bi, si: (si, 0)),
            _const_spec((1, D_MODEL)),
            _const_spec((D_MODEL, IN_COLS)),
            _const_spec((CONV_K, CONV_WIDTH)),
            pl.BlockSpec(memory_space=pltpu.SMEM),
            _const_spec((1, ATTN_WIDTH)),
            _const_spec((1, CONV_WIDTH)),
            _const_spec((D_MODEL, D_MODEL)),
            _const_spec((1, D_MODEL)),
        ],
        out_specs=pl.BlockSpec((None, MIX_TILE, D_MODEL), lambda bi, si: (bi, si, 0)),
        out_shape=jax.ShapeDtypeStruct((b, s, d), x.dtype),
        scratch_shapes=[
            pltpu.VMEM((MIX_TILE + BLOCK, 2 * LANES), bf),
            pltpu.VMEM((MIX_TILE + BLOCK, 2 * LANES), bf),
            pltpu.VMEM((MIX_TILE + CONV_PAD, CONV_WIDTH), jnp.float32),
        ],
        compiler_params=pltpu.CompilerParams(
            dimension_semantics=("arbitrary", "arbitrary"),
            vmem_limit_bytes=VMEM_LIMIT_BYTES),
    )
    h = mixer(x, cosf, sinf, pre_mix_norm, w_in[0].astype(bf), conv_w[0], attn_sinks[0],
              attn_group_norm, conv_group_norm, w_out[0].astype(bf), post_mix_norm)

    n_tok = b * s
    mlp = pl.pallas_call(
        _mlp_kernel,
        name="mlp",
        grid=(n_tok // MLP_TILE,),
        in_specs=[
            pl.BlockSpec((MLP_TILE, D_MODEL), lambda i: (i, 0)),
            _const_spec((1, D_MODEL)),
            _const_spec((D_MODEL, D_FF)),
            _const_spec((D_FF, D_MODEL)),
            _const_spec((1, D_MODEL)),
        ],
        out_specs=pl.BlockSpec((MLP_TILE, D_MODEL), lambda i: (i, 0)),
        out_shape=jax.ShapeDtypeStruct((n_tok, d), x.dtype),
        compiler_params=pltpu.CompilerParams(
            dimension_semantics=("arbitrary",),
            vmem_limit_bytes=VMEM_LIMIT_BYTES),
    )
    out = mlp(h.reshape(n_tok, d), pre_mlp_norm, w_up[0].astype(bf), w_down[0].astype(bf),
              post_mlp_norm)
    return out.reshape(b, s, d)
```

```python
import functools
import math

import jax
import jax.numpy as jnp
from jax import lax
from jax.experimental import pallas as pl
from jax.experimental.pallas import tpu as pltpu

D_MODEL = 1024
HEAD_DIM = 64
ATTN_WIDTH = 512
CONV_WIDTH = 512
N_HEADS = 8
N_KV_HEADS = 2
GROUP = N_HEADS // N_KV_HEADS
KV_WIDTH = N_KV_HEADS * HEAD_DIM
CONV_K = 3
BLOCK = 128
ROPE_THETA = 500000.0
ROT_DIM = HEAD_DIM // 4
D_FF = 4 * D_MODEL
NORM_EPS = 1e-6
NEG_INF = -1e30
IN_COLS = ATTN_WIDTH + 2 * KV_WIDTH + 3 * CONV_WIDTH

LANES = 128
GROUP_WIDTH = GROUP * HEAD_DIM
GROUP_ROWS = GROUP * BLOCK

Q_OFF = 0
K_OFF = ATTN_WIDTH
V_OFF = K_OFF + KV_WIDTH
B_OFF = V_OFF + KV_WIDTH

TILE = 512
FF_CHUNK = 512
CONV_PAD = 8
VMEM_LIMIT_BYTES = 58 * 1024 * 1024


def _rms(x, gain):
    ms = jnp.mean(x * x, axis=-1, keepdims=True)
    return x * lax.rsqrt(ms + NORM_EPS) * gain


def _rope(t, cosf, sinf, lo_half):
    fwd = pltpu.roll(t, LANES - ROT_DIM // 2, axis=1)
    bwd = pltpu.roll(t, ROT_DIM // 2, axis=1)
    return t * cosf + jnp.where(lo_half, fwd, bwd) * sinf


def _layer_kernel(n_tiles, tiles_per_seq,
                  x_ref, cos_ref, sin_ref, g_pre_ref, w_in_ref, conv_w_ref, sinks_ref,
                  g_attn_ref, g_conv_ref, w_out_ref, g_post_ref,
                  w_up_ref, w_down_ref, g_post_mlp_ref, o_ref,
                  kbuf, vbuf, ubuf, hbuf):
    T = x_ref.shape[0]
    nblk = T // BLOCK
    step_idx = pl.program_id(0)
    s_idx = lax.rem(jnp.minimum(step_idx, n_tiles - 1), tiles_per_seq)
    keep = step_idx >= 0

    @pl.when(s_idx == 0)
    def _():
        kbuf[0:BLOCK, :] = jnp.zeros((BLOCK, kbuf.shape[1]), kbuf.dtype)
        vbuf[0:BLOCK, :] = jnp.zeros((BLOCK, vbuf.shape[1]), vbuf.dtype)
        ubuf[0:CONV_PAD, :] = jnp.zeros((CONV_PAD, CONV_WIDTH), ubuf.dtype)

    @pl.when(step_idx == 0)
    def _():
        hbuf[...] = jnp.zeros(hbuf.shape, hbuf.dtype)

    w_off = pl.multiple_of(lax.rem(step_idx, 2) * T, T)
    r_off = pl.multiple_of(lax.rem(step_idx + 1, 2) * T, T)

    h_prev = hbuf[pl.ds(r_off, T), :]
    ms_h = jnp.mean(h_prev * h_prev, axis=-1, keepdims=True)
    hb = h_prev.astype(jnp.bfloat16)
    n_ff = D_FF // FF_CHUNK
    acts = {}
    mlp_acc = [None]

    def up(c):
        z = jnp.dot(hb, w_up_ref[:, c * FF_CHUNK:(c + 1) * FF_CHUNK],
                    preferred_element_type=jnp.float32)
        z = jnp.maximum(z, 0.0)
        acts[c] = (z * z).astype(jnp.bfloat16)

    def down(c):
        part = jnp.dot(acts.pop(c), w_down_ref[c * FF_CHUNK:(c + 1) * FF_CHUNK, :],
                       preferred_element_type=jnp.float32)
        part = jnp.where(keep, part, 0.0)
        mlp_acc[0] = part if mlp_acc[0] is None else mlp_acc[0] + part

    mlp_ops = [(up, 0)]
    for c in range(1, n_ff):
        mlp_ops += [(up, c), (down, c - 1)]
    mlp_ops.append((down, n_ff - 1))

    def emit_mlp(n):
        for _ in range(n):
            if mlp_ops:
                fn, c = mlp_ops.pop(0)
                fn(c)

    emit_mlp(1)
    x = x_ref[...]
    hn = _rms(x, g_pre_ref[...]).astype(jnp.bfloat16)
    proj = jnp.dot(hn, w_in_ref[:, 0:B_OFF], preferred_element_type=jnp.float32)
    emit_mlp(1)
    cproj = jnp.dot(hn, w_in_ref[:, B_OFF:IN_COLS], preferred_element_type=jnp.float32)
    emit_mlp(1)

    lane = lax.broadcasted_iota(jnp.int32, (T, LANES), 1)
    lo_half = lane % HEAD_DIM < ROT_DIM // 2
    first_head = lane < HEAD_DIM
    cosf = cos_ref[...]
    sinf = sin_ref[...]

    scale = 1.0 / math.sqrt(HEAD_DIM)
    q_cols = []
    for c in range(ATTN_WIDTH // LANES):
        qc = proj[:, Q_OFF + c * LANES:Q_OFF + (c + 1) * LANES]
        q_cols.append((_rope(qc, cosf, sinf, lo_half) * scale).astype(jnp.bfloat16))

    k = _rope(proj[:, K_OFF:K_OFF + KV_WIDTH], cosf, sinf, lo_half)
    k_sw = pltpu.roll(k, HEAD_DIM, axis=1)
    kbuf[BLOCK:BLOCK + T, 0:LANES] = jnp.where(first_head, k, k_sw).astype(jnp.bfloat16)
    kbuf[BLOCK:BLOCK + T, LANES:2 * LANES] = jnp.where(first_head, k_sw, k).astype(jnp.bfloat16)

    v = proj[:, V_OFF:V_OFF + KV_WIDTH]
    v_sw = pltpu.roll(v, HEAD_DIM, axis=1)
    vbuf[BLOCK:BLOCK + T, 0:LANES] = jnp.where(first_head, v, v_sw).astype(jnp.bfloat16)
    vbuf[BLOCK:BLOCK + T, LANES:2 * LANES] = jnp.where(first_head, v_sw, v).astype(jnp.bfloat16)

    u = cproj[:, CONV_WIDTH:2 * CONV_WIDTH] * cproj[:, 2 * CONV_WIDTH:3 * CONV_WIDTH]
    ubuf[CONV_PAD:CONV_PAD + T, :] = u
    cw = conv_w_ref[...]
    conv = (ubuf[CONV_PAD - 2:CONV_PAD - 2 + T, :] * cw[0:1, :]
            + ubuf[CONV_PAD - 1:CONV_PAD - 1 + T, :] * cw[1:2, :]
            + u * cw[2:3, :])
    conv = cproj[:, 0:CONV_WIDTH] * conv
    ubuf[0:CONV_PAD, :] = ubuf[T:T + CONV_PAD, :]
    conv_n = _rms(conv, g_conv_ref[...]).astype(jnp.bfloat16)

    srow = lax.broadcasted_iota(jnp.int32, (GROUP_ROWS, BLOCK), 0) % BLOCK
    scol = lax.broadcasted_iota(jnp.int32, (GROUP_ROWS, BLOCK), 1)
    own_side = scol <= srow
    first_pen = jnp.where(s_idx == 0, NEG_INF, 0.0)

    grow = lax.broadcasted_iota(jnp.int32, (GROUP_ROWS, GROUP_WIDTH), 0) // BLOCK
    glane = lax.broadcasted_iota(jnp.int32, (GROUP_ROWS, GROUP_WIDTH), 1) // HEAD_DIM
    own_lanes = grow == glane
    out_head = lax.broadcasted_iota(jnp.int32, (BLOCK, GROUP_WIDTH), 1) // HEAD_DIM
    hrow = lax.broadcasted_iota(jnp.int32, (GROUP_ROWS, 1), 0) // BLOCK
    sink_cols = []
    for g in range(N_KV_HEADS):
        sc = jnp.full((GROUP_ROWS, 1), sinks_ref[g * GROUP], jnp.float32)
        for hh in range(1, GROUP):
            sc = jnp.where(hrow == hh, sinks_ref[g * GROUP + hh], sc)
        sink_cols.append(sc)

    def scores(i, g):
        r0 = i * BLOCK
        qg = jnp.concatenate(
            [q_cols[2 * g][r0:r0 + BLOCK], q_cols[2 * g + 1][r0:r0 + BLOCK]], axis=1)
        qs = jnp.concatenate([qg] * GROUP, axis=0)
        qs = jnp.where(own_lanes, qs, jnp.zeros_like(qs))
        kd = kbuf[r0:r0 + 2 * BLOCK, g * LANES:(g + 1) * LANES]
        krep = jnp.concatenate([kd, kd], axis=1)
        return lax.dot_general(qs, krep, (((1,), (1,)), ((), ())),
                               preferred_element_type=jnp.float32)

    def probs(i, g, s):
        s_prev = s[:, 0:BLOCK]
        if i == 0:
            s_prev = s_prev + first_pen
        s = jnp.where(own_side, s[:, BLOCK:2 * BLOCK], s_prev)
        sink = sink_cols[g]
        m = jnp.max(s, axis=-1, keepdims=True)
        e = jnp.exp(s - m)
        denom = jnp.sum(e, axis=-1, keepdims=True) + jnp.exp(sink - m)
        p = (e * (1.0 / denom)).astype(jnp.bfloat16)
        zeros = jnp.zeros_like(p)
        return jnp.concatenate([jnp.where(own_side, zeros, p), jnp.where(own_side, p, zeros)],
                               axis=1)

    def values(i, g, p):
        r0 = i * BLOCK
        vd = vbuf[r0:r0 + 2 * BLOCK, g * LANES:(g + 1) * LANES]
        vrep = jnp.concatenate([vd, vd], axis=1)
        pv = jnp.dot(p, vrep, preferred_element_type=jnp.float32)
        out = pv[(GROUP - 1) * BLOCK:GROUP * BLOCK]
        for hh in range(GROUP - 2, -1, -1):
            out = jnp.where(out_head == hh, pv[hh * BLOCK:(hh + 1) * BLOCK], out)
        return out

    items = [(i, g) for i in range(nblk) for g in range(N_KV_HEADS)]
    n_items = len(items)
    s_vals, p_vals, outs = {}, {}, {}
    mix_conv_cols = []
    half = D_MODEL // 2
    for step in range(n_items + 2):
        if step < n_items:
            s_vals[step] = scores(*items[step])
        else:
            c0 = (step - n_items) * half
            mix_conv_cols.append(jnp.dot(conv_n, w_out_ref[ATTN_WIDTH:, c0:c0 + half],
                                         preferred_element_type=jnp.float32))
        emit_mlp(1)
        if 0 <= step - 1 < n_items:
            p_vals[step - 1] = probs(*items[step - 1], s_vals.pop(step - 1))
        if 0 <= step - 2 < n_items:
            outs[items[step - 2]] = values(*items[step - 2], p_vals.pop(step - 2))
    attn = jnp.concatenate(
        [jnp.concatenate([outs[(i, g)] for g in range(N_KV_HEADS)], axis=1) for i in range(nblk)],
        axis=0)

    kbuf[0:BLOCK, :] = kbuf[T:T + BLOCK, :]
    vbuf[0:BLOCK, :] = vbuf[T:T + BLOCK, :]

    emit_mlp(1)
    mix_conv = jnp.where(keep, jnp.concatenate(mix_conv_cols, axis=1), 0.0)
    attn_n = _rms(attn, g_attn_ref[...]).astype(jnp.bfloat16)
    mix_out = mix_conv + jnp.dot(attn_n, w_out_ref[0:ATTN_WIDTH, :],
                                 preferred_element_type=jnp.float32)
    emit_mlp(len(mlp_ops))
    hbuf[pl.ds(w_off, T), :] = x + _rms(mix_out, g_post_ref[...])

    acc = mlp_acc[0]
    eps_adj = NORM_EPS * (ms_h + NORM_EPS) * (ms_h + NORM_EPS)
    ms_acc = jnp.mean(acc * acc, axis=-1, keepdims=True)
    o_ref[...] = h_prev + acc * lax.rsqrt(ms_acc + eps_adj) * g_post_mlp_ref[...]


def _rope_lane_tables(seq):
    pos = jnp.arange(seq, dtype=jnp.float32)
    inv_freq = ROPE_THETA ** (-jnp.arange(0, ROT_DIM, 2, dtype=jnp.float32) / ROT_DIM)
    ang = pos[:, None] * inv_freq[None, :]
    cos = jnp.cos(ang)
    sin = jnp.sin(ang)
    pad = HEAD_DIM - ROT_DIM
    cos_head = jnp.concatenate([cos, cos, jnp.ones((seq, pad), jnp.float32)], axis=1)
    sin_head = jnp.concatenate([-sin, sin, jnp.zeros((seq, pad), jnp.float32)], axis=1)
    reps = LANES // HEAD_DIM
    return jnp.tile(cos_head, (1, reps)), jnp.tile(sin_head, (1, reps))


def _const_spec(shape):
    return pl.BlockSpec(shape, lambda *_: (0,) * len(shape))


@jax.jit
def kernel(x, pre_mix_norm, w_in, conv_w, attn_sinks, attn_group_norm, conv_group_norm,
           w_out, post_mix_norm, pre_mlp_norm, w_up, w_down, post_mlp_norm):
    b, s, d = x.shape
    assert d == D_MODEL and s % TILE == 0
    assert pre_mix_norm.shape[0] == 1, "single-layer kernel"
    cosf, sinf = _rope_lane_tables(s)
    bf = jnp.bfloat16
    tiles_per_seq = s // TILE
    n_tiles = b * tiles_per_seq
    n_tok = b * s

    def mix_tile(t):
        return jnp.minimum(t, n_tiles - 1)

    w_up_g = (w_up[0] * pre_mlp_norm[0][:, None]).astype(bf)

    layer = pl.pallas_call(
        functools.partial(_layer_kernel, n_tiles, tiles_per_seq),
        name="layer",
        grid=(n_tiles + 1,),
        in_specs=[
            pl.BlockSpec((TILE, D_MODEL), lambda t: (mix_tile(t), 0)),
            pl.BlockSpec((TILE, LANES), lambda t: (mix_tile(t) % tiles_per_seq, 0)),
            pl.BlockSpec((TILE, LANES), lambda t: (mix_tile(t) % tiles_per_seq, 0)),
            _const_spec((1, D_MODEL)),
            _const_spec((D_MODEL, IN_COLS)),
            _const_spec((CONV_K, CONV_WIDTH)),
            pl.BlockSpec(memory_space=pltpu.SMEM),
            _const_spec((1, ATTN_WIDTH)),
            _const_spec((1, CONV_WIDTH)),
            _const_spec((D_MODEL, D_MODEL)),
            _const_spec((1, D_MODEL)),
            _const_spec((D_MODEL, D_FF)),
            _const_spec((D_FF, D_MODEL)),
            _const_spec((1, D_MODEL)),
        ],
        out_specs=pl.BlockSpec((TILE, D_MODEL), lambda t: (jnp.maximum(t - 1, 0), 0)),
        out_shape=jax.ShapeDtypeStruct((n_tok, d), x.dtype),
        scratch_shapes=[
            pltpu.VMEM((TILE + BLOCK, 2 * LANES), bf),
            pltpu.VMEM((TILE + BLOCK, 2 * LANES), bf),
            pltpu.VMEM((TILE + CONV_PAD, CONV_WIDTH), jnp.float32),
            pltpu.VMEM((2 * TILE, D_MODEL), jnp.float32),
        ],
        compiler_params=pltpu.CompilerParams(
            dimension_semantics=("arbitrary",),
            vmem_limit_bytes=VMEM_LIMIT_BYTES),
    )
    out = layer(x.reshape(n_tok, d), cosf, sinf, pre_mix_norm, w_in[0].astype(bf), conv_w[0],
                attn_sinks[0], attn_group_norm, conv_group_norm, w_out[0].astype(bf),
                post_mix_norm, w_up_g, w_down[0].astype(bf), post_mlp_norm)
    return out.reshape(b, s, d)
```

```python
import functools
import math

import jax
import jax.numpy as jnp
from jax import lax
from jax.experimental import pallas as pl
from jax.experimental.pallas import tpu as pltpu

D_MODEL = 1024
HEAD_DIM = 64
ATTN_WIDTH = 512
CONV_WIDTH = 512
N_HEADS = 8
N_KV_HEADS = 2
GROUP = N_HEADS // N_KV_HEADS
KV_WIDTH = N_KV_HEADS * HEAD_DIM
CONV_K = 3
BLOCK = 128
ROPE_THETA = 500000.0
ROT_DIM = HEAD_DIM // 4
D_FF = 4 * D_MODEL
NORM_EPS = 1e-6
NEG_INF = -1e30
IN_COLS = ATTN_WIDTH + 2 * KV_WIDTH + 3 * CONV_WIDTH

LANES = 128
GROUP_WIDTH = GROUP * HEAD_DIM
GROUP_ROWS = GROUP * BLOCK

Q_OFF = 0
K_OFF = ATTN_WIDTH
V_OFF = K_OFF + KV_WIDTH
B_OFF = V_OFF + KV_WIDTH

TILE = 512
FF_CHUNK = 512
CONV_PAD = 8
VMEM_LIMIT_BYTES = 58 * 1024 * 1024


def _rms(x, gain):
    ms = jnp.mean(x * x, axis=-1, keepdims=True)
    return x * lax.rsqrt(ms + NORM_EPS) * gain


def _rope(t, cosf, sinf, lo_half):
    fwd = pltpu.roll(t, LANES - ROT_DIM // 2, axis=1)
    bwd = pltpu.roll(t, ROT_DIM // 2, axis=1)
    return t * cosf + jnp.where(lo_half, fwd, bwd) * sinf


def _layer_kernel(n_tiles, tiles_per_seq,
                  x_ref, cos_ref, sin_ref, g_pre_ref, w_in_ref, conv_w_ref, sinks_ref,
                  g_attn_ref, g_conv_ref, w_out_ref, g_post_ref,
                  w_up_ref, w_down_ref, g_post_mlp_ref, o_ref,
                  kbuf, vbuf, ubuf, hbuf):
    T = x_ref.shape[0]
    nblk = T // BLOCK
    step_idx = pl.program_id(0)
    s_idx = lax.rem(jnp.minimum(step_idx, n_tiles - 1), tiles_per_seq)
    keep = step_idx >= 0

    @pl.when(s_idx == 0)
    def _():
        kbuf[0:BLOCK, :] = jnp.zeros((BLOCK, kbuf.shape[1]), kbuf.dtype)
        vbuf[0:BLOCK, :] = jnp.zeros((BLOCK, vbuf.shape[1]), vbuf.dtype)
        ubuf[0:CONV_PAD, :] = jnp.zeros((CONV_PAD, CONV_WIDTH), ubuf.dtype)

    @pl.when(step_idx == 0)
    def _():
        hbuf[...] = jnp.zeros(hbuf.shape, hbuf.dtype)

    w_off = pl.multiple_of(lax.rem(step_idx, 2) * T, T)
    r_off = pl.multiple_of(lax.rem(step_idx + 1, 2) * T, T)

    h_prev = hbuf[pl.ds(r_off, T), :]
    ms_h = jnp.mean(h_prev * h_prev, axis=-1, keepdims=True)
    hb = h_prev.astype(jnp.bfloat16)
    n_ff = D_FF // FF_CHUNK
    acts = {}
    mlp_acc = [None]

    def up(c):
        z = jnp.dot(hb, w_up_ref[:, c * FF_CHUNK:(c + 1) * FF_CHUNK],
                    preferred_element_type=jnp.float32)
        z = jnp.maximum(z, 0.0)
        acts[c] = (z * z).astype(jnp.bfloat16)

    def down(c):
        part = jnp.dot(acts.pop(c), w_down_ref[c * FF_CHUNK:(c + 1) * FF_CHUNK, :],
                       preferred_element_type=jnp.float32)
        part = jnp.where(keep, part, 0.0)
        mlp_acc[0] = part if mlp_acc[0] is None else mlp_acc[0] + part

    eps_adj = NORM_EPS * (ms_h + NORM_EPS) * (ms_h + NORM_EPS)

    def down_last(c):
        a = acts.pop(c)
        w = w_down_ref[c * FF_CHUNK:(c + 1) * FF_CHUNK, :]
        for r0 in range(0, T, T // 2):
            rows = slice(r0, r0 + T // 2)
            part = jnp.dot(a[rows], w, preferred_element_type=jnp.float32)
            acc = mlp_acc[0][rows] + part
            ms_acc = jnp.mean(acc * acc, axis=-1, keepdims=True)
            o_ref[rows, :] = (h_prev[rows] + acc * lax.rsqrt(ms_acc + eps_adj[rows])
                              * g_post_mlp_ref[...])

    mlp_ops = [(up, 0)]
    for c in range(1, n_ff):
        mlp_ops += [(up, c), (down, c - 1)]
    mlp_ops.append((down_last, n_ff - 1))

    def emit_mlp(n):
        for _ in range(n):
            if mlp_ops:
                fn, c = mlp_ops.pop(0)
                fn(c)

    emit_mlp(1)
    x = x_ref[...]
    hn = _rms(x, g_pre_ref[...]).astype(jnp.bfloat16)
    proj = jnp.dot(hn, w_in_ref[:, 0:B_OFF], preferred_element_type=jnp.float32)
    emit_mlp(1)
    cproj = jnp.dot(hn, w_in_ref[:, B_OFF:IN_COLS], preferred_element_type=jnp.float32)
    emit_mlp(1)

    lane = lax.broadcasted_iota(jnp.int32, (T, LANES), 1)
    lo_half = lane % HEAD_DIM < ROT_DIM // 2
    first_head = lane < HEAD_DIM
    cosf = cos_ref[...]
    sinf = sin_ref[...]

    scale = 1.0 / math.sqrt(HEAD_DIM)
    q_cols = []
    for c in range(ATTN_WIDTH // LANES):
        qc = proj[:, Q_OFF + c * LANES:Q_OFF + (c + 1) * LANES]
        q_cols.append((_rope(qc, cosf, sinf, lo_half) * scale).astype(jnp.bfloat16))

    k = _rope(proj[:, K_OFF:K_OFF + KV_WIDTH], cosf, sinf, lo_half)
    k_sw = pltpu.roll(k, HEAD_DIM, axis=1)
    kbuf[BLOCK:BLOCK + T, 0:LANES] = jnp.where(first_head, k, k_sw).astype(jnp.bfloat16)
    kbuf[BLOCK:BLOCK + T, LANES:2 * LANES] = jnp.where(first_head, k_sw, k).astype(jnp.bfloat16)

    v = proj[:, V_OFF:V_OFF + KV_WIDTH]
    v_sw = pltpu.roll(v, HEAD_DIM, axis=1)
    vbuf[BLOCK:BLOCK + T, 0:LANES] = jnp.where(first_head, v, v_sw).astype(jnp.bfloat16)
    vbuf[BLOCK:BLOCK + T, LANES:2 * LANES] = jnp.where(first_head, v_sw, v).astype(jnp.bfloat16)

    u = cproj[:, CONV_WIDTH:2 * CONV_WIDTH] * cproj[:, 2 * CONV_WIDTH:3 * CONV_WIDTH]
    ubuf[CONV_PAD:CONV_PAD + T, :] = u
    cw = conv_w_ref[...]
    conv = (ubuf[CONV_PAD - 2:CONV_PAD - 2 + T, :] * cw[0:1, :]
            + ubuf[CONV_PAD - 1:CONV_PAD - 1 + T, :] * cw[1:2, :]
            + u * cw[2:3, :])
    conv = cproj[:, 0:CONV_WIDTH] * conv
    ubuf[0:CONV_PAD, :] = ubuf[T:T + CONV_PAD, :]
    conv_n = _rms(conv, g_conv_ref[...]).astype(jnp.bfloat16)

    srow = lax.broadcasted_iota(jnp.int32, (GROUP_ROWS, BLOCK), 0) % BLOCK
    scol = lax.broadcasted_iota(jnp.int32, (GROUP_ROWS, BLOCK), 1)
    own_side = scol <= srow
    first_pen = jnp.where(s_idx == 0, NEG_INF, 0.0)

    grow = lax.broadcasted_iota(jnp.int32, (GROUP_ROWS, GROUP_WIDTH), 0) // BLOCK
    glane = lax.broadcasted_iota(jnp.int32, (GROUP_ROWS, GROUP_WIDTH), 1) // HEAD_DIM
    own_lanes = grow == glane
    out_head = lax.broadcasted_iota(jnp.int32, (BLOCK, GROUP_WIDTH), 1) // HEAD_DIM
    hrow = lax.broadcasted_iota(jnp.int32, (GROUP_ROWS, 1), 0) // BLOCK
    sink_cols = []
    for g in range(N_KV_HEADS):
        sc = jnp.full((GROUP_ROWS, 1), sinks_ref[g * GROUP], jnp.float32)
        for hh in range(1, GROUP):
            sc = jnp.where(hrow == hh, sinks_ref[g * GROUP + hh], sc)
        sink_cols.append(sc)

    def scores(i, g):
        r0 = i * BLOCK
        qg = jnp.concatenate(
            [q_cols[2 * g][r0:r0 + BLOCK], q_cols[2 * g + 1][r0:r0 + BLOCK]], axis=1)
        qs = jnp.concatenate([qg] * GROUP, axis=0)
        qs = jnp.where(own_lanes, qs, jnp.zeros_like(qs))
        kd = kbuf[r0:r0 + 2 * BLOCK, g * LANES:(g + 1) * LANES]
        krep = jnp.concatenate([kd, kd], axis=1)
        return lax.dot_general(qs, krep, (((1,), (1,)), ((), ())),
                               preferred_element_type=jnp.float32)

    def probs(i, g, s):
        s_prev = s[:, 0:BLOCK]
        if i == 0:
            s_prev = s_prev + first_pen
        s = jnp.where(own_side, s[:, BLOCK:2 * BLOCK], s_prev)
        sink = sink_cols[g]
        m = jnp.max(s, axis=-1, keepdims=True)
        e = jnp.exp(s - m)
        denom = jnp.sum(e, axis=-1, keepdims=True) + jnp.exp(sink - m)
        p = (e * (1.0 / denom)).astype(jnp.bfloat16)
        zeros = jnp.zeros_like(p)
        return jnp.concatenate([jnp.where(own_side, zeros, p), jnp.where(own_side, p, zeros)],
                               axis=1)

    def values(i, g, p):
        r0 = i * BLOCK
        vd = vbuf[r0:r0 + 2 * BLOCK, g * LANES:(g + 1) * LANES]
        vrep = jnp.concatenate([vd, vd], axis=1)
        pv = jnp.dot(p, vrep, preferred_element_type=jnp.float32)
        out = pv[(GROUP - 1) * BLOCK:GROUP * BLOCK]
        for hh in range(GROUP - 2, -1, -1):
            out = jnp.where(out_head == hh, pv[hh * BLOCK:(hh + 1) * BLOCK], out)
        return out

    items = [(i, g) for i in range(nblk) for g in range(N_KV_HEADS)]
    n_items = len(items)
    s_vals, p_vals, outs = {}, {}, {}
    mix_conv_cols = []
    half = D_MODEL // 2
    for step in range(n_items + 2):
        if step < n_items:
            s_vals[step] = scores(*items[step])
        else:
            c0 = (step - n_items) * half
            mix_conv_cols.append(jnp.dot(conv_n, w_out_ref[ATTN_WIDTH:, c0:c0 + half],
                                         preferred_element_type=jnp.float32))
        emit_mlp(1)
        if 0 <= step - 1 < n_items:
            p_vals[step - 1] = probs(*items[step - 1], s_vals.pop(step - 1))
        if 0 <= step - 2 < n_items:
            outs[items[step - 2]] = values(*items[step - 2], p_vals.pop(step - 2))
    attn = jnp.concatenate(
        [jnp.concatenate([outs[(i, g)] for g in range(N_KV_HEADS)], axis=1) for i in range(nblk)],
        axis=0)

    kbuf[0:BLOCK, :] = kbuf[T:T + BLOCK, :]
    vbuf[0:BLOCK, :] = vbuf[T:T + BLOCK, :]

    emit_mlp(1)
    mix_conv = jnp.where(keep, jnp.concatenate(mix_conv_cols, axis=1), 0.0)
    attn_n = _rms(attn, g_attn_ref[...]).astype(jnp.bfloat16)
    mix_out = mix_conv + jnp.dot(attn_n, w_out_ref[0:ATTN_WIDTH, :],
                                 preferred_element_type=jnp.float32)
    emit_mlp(len(mlp_ops))
    hbuf[pl.ds(w_off, T), :] = x + _rms(mix_out, g_post_ref[...])


def _rope_lane_tables(seq):
    pos = jnp.arange(seq, dtype=jnp.float32)
    inv_freq = ROPE_THETA ** (-jnp.arange(0, ROT_DIM, 2, dtype=jnp.float32) / ROT_DIM)
    in_head = jnp.arange(LANES) % HEAD_DIM
    rotary = in_head < ROT_DIM
    freq = jnp.where(rotary, inv_freq[in_head % (ROT_DIM // 2)], 0.0)
    sign = jnp.where(in_head < ROT_DIM // 2, -1.0, 1.0).astype(jnp.float32)
    ang = pos[:, None] * freq[None, :]
    return jnp.cos(ang), jnp.sin(ang) * sign[None, :]


def _const_spec(shape):
    return pl.BlockSpec(shape, lambda *_: (0,) * len(shape))


@jax.jit
def kernel(x, pre_mix_norm, w_in, conv_w, attn_sinks, attn_group_norm, conv_group_norm,
           w_out, post_mix_norm, pre_mlp_norm, w_up, w_down, post_mlp_norm):
    b, s, d = x.shape
    assert d == D_MODEL and s % TILE == 0
    assert pre_mix_norm.shape[0] == 1, "single-layer kernel"
    cosf, sinf = _rope_lane_tables(s)
    bf = jnp.bfloat16
    tiles_per_seq = s // TILE
    n_tiles = b * tiles_per_seq
    n_tok = b * s

    def mix_tile(t):
        return jnp.minimum(t, n_tiles - 1)

    w_up_g = (w_up[0] * pre_mlp_norm[0][:, None]).astype(bf)

    layer = pl.pallas_call(
        functools.partial(_layer_kernel, n_tiles, tiles_per_seq),
        name="layer",
        grid=(n_tiles + 1,),
        in_specs=[
            pl.BlockSpec((TILE, D_MODEL), lambda t: (mix_tile(t), 0)),
            pl.BlockSpec((TILE, LANES), lambda t: (mix_tile(t) % tiles_per_seq, 0)),
            pl.BlockSpec((TILE, LANES), lambda t: (mix_tile(t) % tiles_per_seq, 0)),
            _const_spec((1, D_MODEL)),
            _const_spec((D_MODEL, IN_COLS)),
            _const_spec((CONV_K, CONV_WIDTH)),
            pl.BlockSpec(memory_space=pltpu.SMEM),
            _const_spec((1, ATTN_WIDTH)),
            _const_spec((1, CONV_WIDTH)),
            _const_spec((D_MODEL, D_MODEL)),
            _const_spec((1, D_MODEL)),
            _const_spec((D_MODEL, D_FF)),
            _const_spec((D_FF, D_MODEL)),
            _const_spec((1, D_MODEL)),
        ],
        out_specs=pl.BlockSpec((TILE, D_MODEL), lambda t: (jnp.maximum(t - 1, 0), 0)),
        out_shape=jax.ShapeDtypeStruct((n_tok, d), x.dtype),
        scratch_shapes=[
            pltpu.VMEM((TILE + BLOCK, 2 * LANES), bf),
            pltpu.VMEM((TILE + BLOCK, 2 * LANES), bf),
            pltpu.VMEM((TILE + CONV_PAD, CONV_WIDTH), jnp.float32),
            pltpu.VMEM((2 * TILE, D_MODEL), jnp.float32),
        ],
        compiler_params=pltpu.CompilerParams(
            dimension_semantics=("arbitrary",),
            vmem_limit_bytes=VMEM_LIMIT_BYTES),
    )
    out = layer(x.reshape(n_tok, d), cosf, sinf, pre_mix_norm, w_in[0].astype(bf), conv_w[0],
                attn_sinks[0], attn_group_norm, conv_group_norm, w_out[0].astype(bf),
                post_mix_norm, w_up_g, w_down[0].astype(bf), post_mlp_norm)
    return out.reshape(b, s, d)
```

```python
import functools
import math

import jax
import jax.numpy as jnp
from jax import lax
from jax.experimental import pallas as pl
from jax.experimental.pallas import tpu as pltpu

D_MODEL = 1024
HEAD_DIM = 64
ATTN_WIDTH = 512
CONV_WIDTH = 512
N_HEADS = 8
N_KV_HEADS = 2
GROUP = N_HEADS // N_KV_HEADS
KV_WIDTH = N_KV_HEADS * HEAD_DIM
CONV_K = 3
BLOCK = 128
ROPE_THETA = 500000.0
ROT_DIM = HEAD_DIM // 4
D_FF = 4 * D_MODEL
NORM_EPS = 1e-6
NEG_INF = -1e30
IN_COLS = ATTN_WIDTH + 2 * KV_WIDTH + 3 * CONV_WIDTH

LANES = 128
GROUP_WIDTH = GROUP * HEAD_DIM
GROUP_ROWS = GROUP * BLOCK

Q_OFF = 0
K_OFF = ATTN_WIDTH
V_OFF = K_OFF + KV_WIDTH
B_OFF = V_OFF + KV_WIDTH

TILE = 512
FF_CHUNK = 512
CONV_PAD = 8
VMEM_LIMIT_BYTES = 58 * 1024 * 1024


def _rms(x, gain):
    ms = jnp.mean(x * x, axis=-1, keepdims=True)
    return x * lax.rsqrt(ms + NORM_EPS) * gain


def _rope(t, cosf, sinf, lo_half):
    fwd = pltpu.roll(t, LANES - ROT_DIM // 2, axis=1)
    bwd = pltpu.roll(t, ROT_DIM // 2, axis=1)
    return t * cosf + jnp.where(lo_half, fwd, bwd) * sinf


def _layer_kernel(n_tiles, tiles_per_seq,
                  x_ref, cos_ref, sin_ref, g_pre_ref, w_in_ref, conv_w_ref, sinks_ref,
                  g_attn_ref, g_conv_ref, w_out_ref, g_post_ref,
                  w_up_ref, w_down_ref, g_post_mlp_ref, o_ref,
                  kbuf, vbuf, ubuf, hbuf):
    T = x_ref.shape[0]
    nblk = T // BLOCK
    step_idx = pl.program_id(0)
    s_idx = lax.rem(jnp.minimum(step_idx, n_tiles - 1), tiles_per_seq)
    keep = step_idx >= 0

    @pl.when(s_idx == 0)
    def _():
        kbuf[0:BLOCK, :] = jnp.zeros((BLOCK, kbuf.shape[1]), kbuf.dtype)
        vbuf[0:BLOCK, :] = jnp.zeros((BLOCK, vbuf.shape[1]), vbuf.dtype)
        ubuf[0:CONV_PAD, :] = jnp.zeros((CONV_PAD, CONV_WIDTH), ubuf.dtype)

    @pl.when(step_idx == 0)
    def _():
        hbuf[...] = jnp.zeros(hbuf.shape, hbuf.dtype)

    w_off = pl.multiple_of(lax.rem(step_idx, 2) * T, T)
    r_off = pl.multiple_of(lax.rem(step_idx + 1, 2) * T, T)

    h_prev = hbuf[pl.ds(r_off, T), :]
    hb = h_prev.astype(jnp.bfloat16)
    r_sq = 1.0 / (jnp.mean(h_prev * h_prev, axis=-1, keepdims=True) + NORM_EPS)
    n_ff = D_FF // FF_CHUNK
    acts = {}
    mlp_acc = [None]

    def up(c):
        z = jnp.dot(hb, w_up_ref[:, c * FF_CHUNK:(c + 1) * FF_CHUNK],
                    preferred_element_type=jnp.float32)
        z = jnp.maximum(z, 0.0)
        acts[c] = (z * z * r_sq).astype(jnp.bfloat16)

    def down(c):
        part = jnp.dot(acts.pop(c), w_down_ref[c * FF_CHUNK:(c + 1) * FF_CHUNK, :],
                       preferred_element_type=jnp.float32)
        part = jnp.where(keep, part, 0.0)
        mlp_acc[0] = part if mlp_acc[0] is None else mlp_acc[0] + part

    def down_last(c):
        a = acts.pop(c)
        w = w_down_ref[c * FF_CHUNK:(c + 1) * FF_CHUNK, :]
        for r0 in range(0, T, T // 2):
            rows = slice(r0, r0 + T // 2)
            part = jnp.dot(a[rows], w, preferred_element_type=jnp.float32)
            acc = mlp_acc[0][rows] + part
            o_ref[rows, :] = h_prev[rows] + _rms(acc, g_post_mlp_ref[...])

    mlp_ops = [(up, 0)]
    for c in range(1, n_ff):
        mlp_ops += [(up, c), (down, c - 1)]
    mlp_ops.append((down_last, n_ff - 1))

    def emit_mlp(n):
        for _ in range(n):
            if mlp_ops:
                fn, c = mlp_ops.pop(0)
                fn(c)

    emit_mlp(1)
    x = x_ref[...]
    hn = _rms(x, g_pre_ref[...]).astype(jnp.bfloat16)
    proj = jnp.dot(hn, w_in_ref[:, 0:B_OFF], preferred_element_type=jnp.float32)
    emit_mlp(1)
    cproj = jnp.dot(hn, w_in_ref[:, B_OFF:IN_COLS], preferred_element_type=jnp.float32)
    emit_mlp(1)

    lane = lax.broadcasted_iota(jnp.int32, (T, LANES), 1)
    lo_half = lane % HEAD_DIM < ROT_DIM // 2
    first_head = lane < HEAD_DIM
    cosf = cos_ref[...]
    sinf = sin_ref[...]

    scale = 1.0 / math.sqrt(HEAD_DIM)
    q_cols = []
    for c in range(ATTN_WIDTH // LANES):
        qc = proj[:, Q_OFF + c * LANES:Q_OFF + (c + 1) * LANES]
        q_cols.append((_rope(qc, cosf, sinf, lo_half) * scale).astype(jnp.bfloat16))

    k = _rope(proj[:, K_OFF:K_OFF + KV_WIDTH], cosf, sinf, lo_half)
    k_sw = pltpu.roll(k, HEAD_DIM, axis=1)
    kbuf[BLOCK:BLOCK + T, 0:LANES] = jnp.where(first_head, k, k_sw).astype(jnp.bfloat16)
    kbuf[BLOCK:BLOCK + T, LANES:2 * LANES] = jnp.where(first_head, k_sw, k).astype(jnp.bfloat16)

    v = proj[:, V_OFF:V_OFF + KV_WIDTH]
    v_sw = pltpu.roll(v, HEAD_DIM, axis=1)
    vbuf[BLOCK:BLOCK + T, 0:LANES] = jnp.where(first_head, v, v_sw).astype(jnp.bfloat16)
    vbuf[BLOCK:BLOCK + T, LANES:2 * LANES] = jnp.where(first_head, v_sw, v).astype(jnp.bfloat16)

    u = cproj[:, CONV_WIDTH:2 * CONV_WIDTH] * cproj[:, 2 * CONV_WIDTH:3 * CONV_WIDTH]
    ubuf[CONV_PAD:CONV_PAD + T, :] = u
    cw = conv_w_ref[...]
    conv = (ubuf[CONV_PAD - 2:CONV_PAD - 2 + T, :] * cw[0:1, :]
            + ubuf[CONV_PAD - 1:CONV_PAD - 1 + T, :] * cw[1:2, :]
            + u * cw[2:3, :])
    conv = cproj[:, 0:CONV_WIDTH] * conv
    ubuf[0:CONV_PAD, :] = ubuf[T:T + CONV_PAD, :]
    conv_n = _rms(conv, g_conv_ref[...]).astype(jnp.bfloat16)

    srow = lax.broadcasted_iota(jnp.int32, (GROUP_ROWS, BLOCK), 0) % BLOCK
    scol = lax.broadcasted_iota(jnp.int32, (GROUP_ROWS, BLOCK), 1)
    own_side = scol <= srow
    first_pen = jnp.where(s_idx == 0, NEG_INF, 0.0)

    grow = lax.broadcasted_iota(jnp.int32, (GROUP_ROWS, GROUP_WIDTH), 0) // BLOCK
    glane = lax.broadcasted_iota(jnp.int32, (GROUP_ROWS, GROUP_WIDTH), 1) // HEAD_DIM
    own_lanes = grow == glane
    out_head = lax.broadcasted_iota(jnp.int32, (BLOCK, GROUP_WIDTH), 1) // HEAD_DIM
    hrow = lax.broadcasted_iota(jnp.int32, (GROUP_ROWS, 1), 0) // BLOCK
    sink_cols = []
    for g in range(N_KV_HEADS):
        sc = jnp.full((GROUP_ROWS, 1), sinks_ref[g * GROUP], jnp.float32)
        for hh in range(1, GROUP):
            sc = jnp.where(hrow == hh, sinks_ref[g * GROUP + hh], sc)
        sink_cols.append(sc)

    def scores(i, g):
        r0 = i * BLOCK
        qg = jnp.concatenate(
            [q_cols[2 * g][r0:r0 + BLOCK], q_cols[2 * g + 1][r0:r0 + BLOCK]], axis=1)
        qs = jnp.concatenate([qg] * GROUP, axis=0)
        qs = jnp.where(own_lanes, qs, jnp.zeros_like(qs))
        kd = kbuf[r0:r0 + 2 * BLOCK, g * LANES:(g + 1) * LANES]
        krep = jnp.concatenate([kd, kd], axis=1)
        return lax.dot_general(qs, krep, (((1,), (1,)), ((), ())),
                               preferred_element_type=jnp.float32)

    def probs(i, g, s):
        s_prev = s[:, 0:BLOCK]
        if i == 0:
            s_prev = s_prev + first_pen
        s = jnp.where(own_side, s[:, BLOCK:2 * BLOCK], s_prev)
        sink = sink_cols[g]
        m = jnp.max(s, axis=-1, keepdims=True)
        e = jnp.exp(s - m)
        denom = jnp.sum(e, axis=-1, keepdims=True) + jnp.exp(sink - m)
        p = (e * (1.0 / denom)).astype(jnp.bfloat16)
        zeros = jnp.zeros_like(p)
        return jnp.concatenate([jnp.where(own_side, zeros, p), jnp.where(own_side, p, zeros)],
                               axis=1)

    def values(i, g, p):
        r0 = i * BLOCK
        vd = vbuf[r0:r0 + 2 * BLOCK, g * LANES:(g + 1) * LANES]
        vrep = jnp.concatenate([vd, vd], axis=1)
        pv = jnp.dot(p, vrep, preferred_element_type=jnp.float32)
        out = pv[(GROUP - 1) * BLOCK:GROUP * BLOCK]
        for hh in range(GROUP - 2, -1, -1):
            out = jnp.where(out_head == hh, pv[hh * BLOCK:(hh + 1) * BLOCK], out)
        return out

    items = [(i, g) for i in range(nblk) for g in range(N_KV_HEADS)]
    n_items = len(items)
    s_vals, p_vals, outs = {}, {}, {}
    mix_conv_cols = []
    half = D_MODEL // 2
    for step in range(n_items + 2):
        if step < n_items:
            s_vals[step] = scores(*items[step])
        else:
            c0 = (step - n_items) * half
            mix_conv_cols.append(jnp.dot(conv_n, w_out_ref[ATTN_WIDTH:, c0:c0 + half],
                                         preferred_element_type=jnp.float32))
        emit_mlp(1)
        if 0 <= step - 1 < n_items:
            p_vals[step - 1] = probs(*items[step - 1], s_vals.pop(step - 1))
        if 0 <= step - 2 < n_items:
            outs[items[step - 2]] = values(*items[step - 2], p_vals.pop(step - 2))
    attn = jnp.concatenate(
        [jnp.concatenate([outs[(i, g)] for g in range(N_KV_HEADS)], axis=1) for i in range(nblk)],
        axis=0)

    kbuf[0:BLOCK, :] = kbuf[T:T + BLOCK, :]
    vbuf[0:BLOCK, :] = vbuf[T:T + BLOCK, :]

    emit_mlp(1)
    mix_conv = jnp.where(keep, jnp.concatenate(mix_conv_cols, axis=1), 0.0)
    attn_n = _rms(attn, g_attn_ref[...]).astype(jnp.bfloat16)
    mix_out = mix_conv + jnp.dot(attn_n, w_out_ref[0:ATTN_WIDTH, :],
                                 preferred_element_type=jnp.float32)
    emit_mlp(len(mlp_ops))
    hbuf[pl.ds(w_off, T), :] = x + _rms(mix_out, g_post_ref[...])


def _rope_lane_tables(seq):
    pos = jnp.arange(seq, dtype=jnp.float32)
    inv_freq = ROPE_THETA ** (-jnp.arange(0, ROT_DIM, 2, dtype=jnp.float32) / ROT_DIM)
    ang = pos[:, None] * inv_freq[None, :]
    cos = jnp.cos(ang)
    sin = jnp.sin(ang)
    pad = HEAD_DIM - ROT_DIM
    cos_head = jnp.concatenate([cos, cos, jnp.ones((seq, pad), jnp.float32)], axis=1)
    sin_head = jnp.concatenate([-sin, sin, jnp.zeros((seq, pad), jnp.float32)], axis=1)
    reps = LANES // HEAD_DIM
    return jnp.tile(cos_head, (1, reps)), jnp.tile(sin_head, (1, reps))


def _const_spec(shape):
    return pl.BlockSpec(shape, lambda *_: (0,) * len(shape))


@jax.jit
def kernel(x, pre_mix_norm, w_in, conv_w, attn_sinks, attn_group_norm, conv_group_norm,
           w_out, post_mix_norm, pre_mlp_norm, w_up, w_down, post_mlp_norm):
    b, s, d = x.shape
    assert d == D_MODEL and s % TILE == 0
    assert pre_mix_norm.shape[0] == 1, "single-layer kernel"
    cosf, sinf = _rope_lane_tables(s)
    bf = jnp.bfloat16
    tiles_per_seq = s // TILE
    n_tiles = b * tiles_per_seq
    n_tok = b * s

    def mix_tile(t):
        return jnp.minimum(t, n_tiles - 1)

    w_up_g = (w_up[0] * pre_mlp_norm[0][:, None]).astype(bf)

    layer = pl.pallas_call(
        functools.partial(_layer_kernel, n_tiles, tiles_per_seq),
        name="layer",
        grid=(n_tiles + 1,),
        in_specs=[
            pl.BlockSpec((TILE, D_MODEL), lambda t: (mix_tile(t), 0)),
            pl.BlockSpec((TILE, LANES), lambda t: (mix_tile(t) % tiles_per_seq, 0)),
            pl.BlockSpec((TILE, LANES), lambda t: (mix_tile(t) % tiles_per_seq, 0)),
            _const_spec((1, D_MODEL)),
            _const_spec((D_MODEL, IN_COLS)),
            _const_spec((CONV_K, CONV_WIDTH)),
            pl.BlockSpec(memory_space=pltpu.SMEM),
            _const_spec((1, ATTN_WIDTH)),
            _const_spec((1, CONV_WIDTH)),
            _const_spec((D_MODEL, D_MODEL)),
            _const_spec((1, D_MODEL)),
            _const_spec((D_MODEL, D_FF)),
            _const_spec((D_FF, D_MODEL)),
            _const_spec((1, D_MODEL)),
        ],
        out_specs=pl.BlockSpec((TILE, D_MODEL), lambda t: (jnp.maximum(t - 1, 0), 0)),
        out_shape=jax.ShapeDtypeStruct((n_tok, d), x.dtype),
        scratch_shapes=[
            pltpu.VMEM((TILE + BLOCK, 2 * LANES), bf),
            pltpu.VMEM((TILE + BLOCK, 2 * LANES), bf),
            pltpu.VMEM((TILE + CONV_PAD, CONV_WIDTH), jnp.float32),
            pltpu.VMEM((2 * TILE, D_MODEL), jnp.float32),
        ],
        compiler_params=pltpu.CompilerParams(
            dimension_semantics=("arbitrary",),
            vmem_limit_bytes=VMEM_LIMIT_BYTES),
    )
    out = layer(x.reshape(n_tok, d), cosf, sinf, pre_mix_norm, w_in[0].astype(bf), conv_w[0],
                attn_sinks[0], attn_group_norm, conv_group_norm, w_out[0].astype(bf),
                post_mix_norm, w_up_g, w_down[0].astype(bf), post_mlp_norm)
    return out.reshape(b, s, d)
```

```python
import functools
import math

import jax
import jax.numpy as jnp
from jax import lax
from jax.experimental import pallas as pl
from jax.experimental.pallas import tpu as pltpu

D_MODEL = 1024
HEAD_DIM = 64
ATTN_WIDTH = 512
CONV_WIDTH = 512
N_HEADS = 8
N_KV_HEADS = 2
GROUP = N_HEADS // N_KV_HEADS
KV_WIDTH = N_KV_HEADS * HEAD_DIM
CONV_K = 3
BLOCK = 128
ROPE_THETA = 500000.0
ROT_DIM = HEAD_DIM // 4
D_FF = 4 * D_MODEL
NORM_EPS = 1e-6
NEG_INF = -1e30
IN_COLS = ATTN_WIDTH + 2 * KV_WIDTH + 3 * CONV_WIDTH

LANES = 128
GROUP_WIDTH = GROUP * HEAD_DIM
GROUP_ROWS = GROUP * BLOCK

Q_OFF = 0
K_OFF = ATTN_WIDTH
V_OFF = K_OFF + KV_WIDTH
B_OFF = V_OFF + KV_WIDTH

TILE = 512
FF_CHUNK = 512
CONV_PAD = 8
VMEM_LIMIT_BYTES = 58 * 1024 * 1024


def _rms(x, gain):
    ms = jnp.mean(x * x, axis=-1, keepdims=True)
    return x * lax.rsqrt(ms + NORM_EPS) * gain


def _rope(t, cosf, sinf, lo_half):
    fwd = pltpu.roll(t, LANES - ROT_DIM // 2, axis=1)
    bwd = pltpu.roll(t, ROT_DIM // 2, axis=1)
    return t * cosf + jnp.where(lo_half, fwd, bwd) * sinf


def _layer_kernel(n_tiles, tiles_per_seq,
                  x_ref, rope_ref, g_pre_ref, w_in_ref, conv_w_ref, sinks_ref,
                  g_attn_ref, g_conv_ref, w_out_ref, g_post_ref,
                  w_up_ref, w_down_ref, g_post_mlp_ref, o_ref,
                  kbuf, vbuf, ubuf, hbuf):
    T = x_ref.shape[0]
    nblk = T // BLOCK
    step_idx = pl.program_id(0)
    s_idx = lax.rem(jnp.minimum(step_idx, n_tiles - 1), tiles_per_seq)
    keep = step_idx >= 0

    @pl.when(s_idx == 0)
    def _():
        kbuf[0:BLOCK, :] = jnp.zeros((BLOCK, kbuf.shape[1]), kbuf.dtype)
        vbuf[0:BLOCK, :] = jnp.zeros((BLOCK, vbuf.shape[1]), vbuf.dtype)
        ubuf[0:CONV_PAD, :] = jnp.zeros((CONV_PAD, CONV_WIDTH), ubuf.dtype)

    @pl.when(step_idx == 0)
    def _():
        hbuf[...] = jnp.zeros(hbuf.shape, hbuf.dtype)

    w_off = pl.multiple_of(lax.rem(step_idx, 2) * T, T)
    r_off = pl.multiple_of(lax.rem(step_idx + 1, 2) * T, T)

    h_prev = hbuf[pl.ds(r_off, T), :]
    hb = h_prev.astype(jnp.bfloat16)
    r_sq = 1.0 / (jnp.mean(h_prev * h_prev, axis=-1, keepdims=True) + NORM_EPS)
    n_ff = D_FF // FF_CHUNK
    acts = {}
    mlp_acc = [None]

    def up(c):
        z = jnp.dot(hb, w_up_ref[:, c * FF_CHUNK:(c + 1) * FF_CHUNK],
                    preferred_element_type=jnp.float32)
        z = jnp.maximum(z, 0.0)
        acts[c] = (z * z * r_sq).astype(jnp.bfloat16)

    def down(c):
        part = jnp.dot(acts.pop(c), w_down_ref[c * FF_CHUNK:(c + 1) * FF_CHUNK, :],
                       preferred_element_type=jnp.float32)
        part = jnp.where(keep, part, 0.0)
        mlp_acc[0] = part if mlp_acc[0] is None else mlp_acc[0] + part

    def down_last(c):
        a = acts.pop(c)
        w = w_down_ref[c * FF_CHUNK:(c + 1) * FF_CHUNK, :]
        for r0 in range(0, T, T // 2):
            rows = slice(r0, r0 + T // 2)
            part = jnp.dot(a[rows], w, preferred_element_type=jnp.float32)
            acc = mlp_acc[0][rows] + part
            o_ref[rows, :] = h_prev[rows] + _rms(acc, g_post_mlp_ref[...])

    mlp_ops = [(up, 0)]
    for c in range(1, n_ff):
        mlp_ops += [(up, c), (down, c - 1)]
    mlp_ops.append((down_last, n_ff - 1))

    def emit_mlp(n):
        for _ in range(n):
            if mlp_ops:
                fn, c = mlp_ops.pop(0)
                fn(c)

    emit_mlp(1)
    x = x_ref[...]
    hn = _rms(x, g_pre_ref[...]).astype(jnp.bfloat16)
    proj = jnp.dot(hn, w_in_ref[:, 0:B_OFF], preferred_element_type=jnp.float32)
    emit_mlp(1)
    cproj = jnp.dot(hn, w_in_ref[:, B_OFF:IN_COLS], preferred_element_type=jnp.float32)
    emit_mlp(1)

    lane = lax.broadcasted_iota(jnp.int32, (T, LANES), 1)
    lo_half = lane % HEAD_DIM < ROT_DIM // 2
    first_head = lane < HEAD_DIM
    cosf = rope_ref[:, 0:LANES]
    sinf = rope_ref[:, LANES:2 * LANES]

    scale = 1.0 / math.sqrt(HEAD_DIM)
    q_cols = []
    for c in range(ATTN_WIDTH // LANES):
        qc = proj[:, Q_OFF + c * LANES:Q_OFF + (c + 1) * LANES]
        q_cols.append((_rope(qc, cosf, sinf, lo_half) * scale).astype(jnp.bfloat16))

    k = _rope(proj[:, K_OFF:K_OFF + KV_WIDTH], cosf, sinf, lo_half)
    k_sw = pltpu.roll(k, HEAD_DIM, axis=1)
    kbuf[BLOCK:BLOCK + T, 0:LANES] = jnp.where(first_head, k, k_sw).astype(jnp.bfloat16)
    kbuf[BLOCK:BLOCK + T, LANES:2 * LANES] = jnp.where(first_head, k_sw, k).astype(jnp.bfloat16)

    v = proj[:, V_OFF:V_OFF + KV_WIDTH]
    v_sw = pltpu.roll(v, HEAD_DIM, axis=1)
    vbuf[BLOCK:BLOCK + T, 0:LANES] = jnp.where(first_head, v, v_sw).astype(jnp.bfloat16)
    vbuf[BLOCK:BLOCK + T, LANES:2 * LANES] = jnp.where(first_head, v_sw, v).astype(jnp.bfloat16)

    u = cproj[:, CONV_WIDTH:2 * CONV_WIDTH] * cproj[:, 2 * CONV_WIDTH:3 * CONV_WIDTH]
    ubuf[CONV_PAD:CONV_PAD + T, :] = u
    cw = conv_w_ref[...]
    conv = (ubuf[CONV_PAD - 2:CONV_PAD - 2 + T, :] * cw[0:1, :]
            + ubuf[CONV_PAD - 1:CONV_PAD - 1 + T, :] * cw[1:2, :]
            + u * cw[2:3, :])
    conv = cproj[:, 0:CONV_WIDTH] * conv
    ubuf[0:CONV_PAD, :] = ubuf[T:T + CONV_PAD, :]
    conv_n = _rms(conv, g_conv_ref[...]).astype(jnp.bfloat16)

    srow = lax.broadcasted_iota(jnp.int32, (GROUP_ROWS, BLOCK), 0) % BLOCK
    scol = lax.broadcasted_iota(jnp.int32, (GROUP_ROWS, BLOCK), 1)
    own_side = scol <= srow
    first_pen = jnp.where(s_idx == 0, NEG_INF, 0.0)

    grow = lax.broadcasted_iota(jnp.int32, (GROUP_ROWS, GROUP_WIDTH), 0) // BLOCK
    glane = lax.broadcasted_iota(jnp.int32, (GROUP_ROWS, GROUP_WIDTH), 1) // HEAD_DIM
    own_lanes = grow == glane
    out_head = lax.broadcasted_iota(jnp.int32, (BLOCK, GROUP_WIDTH), 1) // HEAD_DIM
    hrow = lax.broadcasted_iota(jnp.int32, (GROUP_ROWS, 1), 0) // BLOCK
    sink_cols = []
    for g in range(N_KV_HEADS):
        sc = jnp.full((GROUP_ROWS, 1), sinks_ref[g * GROUP], jnp.float32)
        for hh in range(1, GROUP):
            sc = jnp.where(hrow == hh, sinks_ref[g * GROUP + hh], sc)
        sink_cols.append(sc)

    def scores(i, g):
        r0 = i * BLOCK
        qg = jnp.concatenate(
            [q_cols[2 * g][r0:r0 + BLOCK], q_cols[2 * g + 1][r0:r0 + BLOCK]], axis=1)
        qs = jnp.concatenate([qg] * GROUP, axis=0)
        qs = jnp.where(own_lanes, qs, jnp.zeros_like(qs))
        kd = kbuf[r0:r0 + 2 * BLOCK, g * LANES:(g + 1) * LANES]
        krep = jnp.concatenate([kd, kd], axis=1)
        return lax.dot_general(qs, krep, (((1,), (1,)), ((), ())),
                               preferred_element_type=jnp.float32)

    def probs(i, g, s):
        s_prev = s[:, 0:BLOCK]
        if i == 0:
            s_prev = s_prev + first_pen
        s = jnp.where(own_side, s[:, BLOCK:2 * BLOCK], s_prev)
        sink = sink_cols[g]
        m = jnp.max(s, axis=-1, keepdims=True)
        e = jnp.exp(s - m)
        denom = jnp.sum(e, axis=-1, keepdims=True) + jnp.exp(sink - m)
        p = (e * (1.0 / denom)).astype(jnp.bfloat16)
        zeros = jnp.zeros_like(p)
        return jnp.concatenate([jnp.where(own_side, zeros, p), jnp.where(own_side, p, zeros)],
                               axis=1)

    def values(i, g, p):
        r0 = i * BLOCK
        vd = vbuf[r0:r0 + 2 * BLOCK, g * LANES:(g + 1) * LANES]
        vrep = jnp.concatenate([vd, vd], axis=1)
        pv = jnp.dot(p, vrep, preferred_element_type=jnp.float32)
        out = pv[(GROUP - 1) * BLOCK:GROUP * BLOCK]
        for hh in range(GROUP - 2, -1, -1):
            out = jnp.where(out_head == hh, pv[hh * BLOCK:(hh + 1) * BLOCK], out)
        return out

    items = [(i, g) for i in range(nblk) for g in range(N_KV_HEADS)]
    n_items = len(items)
    s_vals, p_vals, outs = {}, {}, {}
    mix_conv_cols = []
    half = D_MODEL // 2
    for step in range(n_items + 2):
        if step < n_items:
            s_vals[step] = scores(*items[step])
        else:
            c0 = (step - n_items) * half
            mix_conv_cols.append(jnp.dot(conv_n, w_out_ref[ATTN_WIDTH:, c0:c0 + half],
                                         preferred_element_type=jnp.float32))
        emit_mlp(1)
        if 0 <= step - 1 < n_items:
            p_vals[step - 1] = probs(*items[step - 1], s_vals.pop(step - 1))
        if 0 <= step - 2 < n_items:
            outs[items[step - 2]] = values(*items[step - 2], p_vals.pop(step - 2))
    attn = jnp.concatenate(
        [jnp.concatenate([outs[(i, g)] for g in range(N_KV_HEADS)], axis=1) for i in range(nblk)],
        axis=0)

    kbuf[0:BLOCK, :] = kbuf[T:T + BLOCK, :]
    vbuf[0:BLOCK, :] = vbuf[T:T + BLOCK, :]

    emit_mlp(1)
    mix_conv = jnp.where(keep, jnp.concatenate(mix_conv_cols, axis=1), 0.0)
    attn_n = _rms(attn, g_attn_ref[...]).astype(jnp.bfloat16)
    mix_out = mix_conv + jnp.dot(attn_n, w_out_ref[0:ATTN_WIDTH, :],
                                 preferred_element_type=jnp.float32)
    emit_mlp(len(mlp_ops))
    hbuf[pl.ds(w_off, T), :] = x + _rms(mix_out, g_post_ref[...])


def _rope_lane_tables(seq):
    pos = jnp.arange(seq, dtype=jnp.float32)
    inv_freq = ROPE_THETA ** (-jnp.arange(0, ROT_DIM, 2, dtype=jnp.float32) / ROT_DIM)
    ang = pos[:, None] * inv_freq[None, :]
    cos = jnp.cos(ang)
    sin = jnp.sin(ang)
    pad = HEAD_DIM - ROT_DIM
    cos_head = jnp.concatenate([cos, cos, jnp.ones((seq, pad), jnp.float32)], axis=1)
    sin_head = jnp.concatenate([-sin, sin, jnp.zeros((seq, pad), jnp.float32)], axis=1)
    reps = LANES // HEAD_DIM
    return jnp.tile(cos_head, (1, reps)), jnp.tile(sin_head, (1, reps))


def _const_spec(shape):
    return pl.BlockSpec(shape, lambda *_: (0,) * len(shape))


@jax.jit
def kernel(x, pre_mix_norm, w_in, conv_w, attn_sinks, attn_group_norm, conv_group_norm,
           w_out, post_mix_norm, pre_mlp_norm, w_up, w_down, post_mlp_norm):
    b, s, d = x.shape
    assert d == D_MODEL and s % TILE == 0
    assert pre_mix_norm.shape[0] == 1, "single-layer kernel"
    rope = jnp.concatenate(_rope_lane_tables(s), axis=1)
    bf = jnp.bfloat16
    tiles_per_seq = s // TILE
    n_tiles = b * tiles_per_seq
    n_tok = b * s

    def mix_tile(t):
        return jnp.minimum(t, n_tiles - 1)

    w_up_g = (w_up[0] * pre_mlp_norm[0][:, None]).astype(bf)

    layer = pl.pallas_call(
        functools.partial(_layer_kernel, n_tiles, tiles_per_seq),
        name="layer",
        grid=(n_tiles + 1,),
        in_specs=[
            pl.BlockSpec((TILE, D_MODEL), lambda t: (mix_tile(t), 0)),
            pl.BlockSpec((TILE, 2 * LANES), lambda t: (mix_tile(t) % tiles_per_seq, 0)),
            _const_spec((1, D_MODEL)),
            _const_spec((D_MODEL, IN_COLS)),
            _const_spec((CONV_K, CONV_WIDTH)),
            pl.BlockSpec(memory_space=pltpu.SMEM),
            _const_spec((1, ATTN_WIDTH)),
            _const_spec((1, CONV_WIDTH)),
            _const_spec((D_MODEL, D_MODEL)),
            _const_spec((1, D_MODEL)),
            _const_spec((D_MODEL, D_FF)),
            _const_spec((D_FF, D_MODEL)),
            _const_spec((1, D_MODEL)),
        ],
        out_specs=pl.BlockSpec((TILE, D_MODEL), lambda t: (jnp.maximum(t - 1, 0), 0)),
        out_shape=jax.ShapeDtypeStruct((n_tok, d), x.dtype),
        scratch_shapes=[
            pltpu.VMEM((TILE + BLOCK, 2 * LANES), bf),
            pltpu.VMEM((TILE + BLOCK, 2 * LANES), bf),
            pltpu.VMEM((TILE + CONV_PAD, CONV_WIDTH), jnp.float32),
            pltpu.VMEM((2 * TILE, D_MODEL), jnp.float32),
        ],
        compiler_params=pltpu.CompilerParams(
            dimension_semantics=("arbitrary",),
            vmem_limit_bytes=VMEM_LIMIT_BYTES),
    )
    out = layer(x.reshape(n_tok, d), rope, pre_mix_norm, w_in[0].astype(bf), conv_w[0],
                attn_sinks[0], attn_group_norm, conv_group_norm, w_out[0].astype(bf),
                post_mix_norm, w_up_g, w_down[0].astype(bf), post_mlp_norm)
    return out.reshape(b, s, d)
```

```python
import functools
import math

import jax
import jax.numpy as jnp
from jax import lax
from jax.experimental import pallas as pl
from jax.experimental.pallas import tpu as pltpu

D_MODEL = 1024
HEAD_DIM = 64
ATTN_WIDTH = 512
CONV_WIDTH = 512
N_HEADS = 8
N_KV_HEADS = 2
GROUP = N_HEADS // N_KV_HEADS
KV_WIDTH = N_KV_HEADS * HEAD_DIM
CONV_K = 3
BLOCK = 128
ROPE_THETA = 500000.0
ROT_DIM = HEAD_DIM // 4
D_FF = 4 * D_MODEL
NORM_EPS = 1e-6
NEG_INF = -1e30
IN_COLS = ATTN_WIDTH + 2 * KV_WIDTH + 3 * CONV_WIDTH

LANES = 128
GROUP_WIDTH = GROUP * HEAD_DIM
GROUP_ROWS = GROUP * BLOCK

Q_OFF = 0
K_OFF = ATTN_WIDTH
V_OFF = K_OFF + KV_WIDTH
B_OFF = V_OFF + KV_WIDTH

TILE = 512
FF_CHUNK = 512
CONV_PAD = 8
VMEM_LIMIT_BYTES = 58 * 1024 * 1024


def _rms(x, gain):
    ms = jnp.mean(x * x, axis=-1, keepdims=True)
    return x * lax.rsqrt(ms + NORM_EPS) * gain


def _rope(t, cosf, sinf, lo_half):
    fwd = pltpu.roll(t, LANES - ROT_DIM // 2, axis=1)
    bwd = pltpu.roll(t, ROT_DIM // 2, axis=1)
    return t * cosf + jnp.where(lo_half, fwd, bwd) * sinf


def _layer_kernel(n_tiles, tiles_per_seq,
                  x_ref, cos_ref, sin_ref, g_pre_ref, w_in_ref, conv_w_ref, sinks_ref,
                  g_attn_ref, g_conv_ref, w_out_ref, g_post_ref,
                  w_up_ref, w_down_ref, g_post_mlp_ref, o_ref,
                  kbuf, vbuf, ubuf, hbuf):
    T = x_ref.shape[0]
    nblk = T // BLOCK
    step_idx = pl.program_id(0)
    s_idx = lax.rem(jnp.minimum(step_idx, n_tiles - 1), tiles_per_seq)
    keep = step_idx >= 0

    @pl.when(s_idx == 0)
    def _():
        kbuf[0:BLOCK, :] = jnp.zeros((BLOCK, kbuf.shape[1]), kbuf.dtype)
        vbuf[0:BLOCK, :] = jnp.zeros((BLOCK, vbuf.shape[1]), vbuf.dtype)
        ubuf[0:CONV_PAD, :] = jnp.zeros((CONV_PAD, CONV_WIDTH), ubuf.dtype)

    @pl.when(step_idx == 0)
    def _():
        hbuf[...] = jnp.zeros(hbuf.shape, hbuf.dtype)

    w_off = pl.multiple_of(lax.rem(step_idx, 2) * T, T)
    r_off = pl.multiple_of(lax.rem(step_idx + 1, 2) * T, T)

    h_prev = hbuf[pl.ds(r_off, T), :]
    hb = h_prev.astype(jnp.bfloat16)
    r_sq = 1.0 / (jnp.mean(h_prev * h_prev, axis=-1, keepdims=True) + NORM_EPS)
    n_ff = D_FF // FF_CHUNK
    acts = {}
    mlp_acc = [None]

    def up(c):
        z = jnp.dot(hb, w_up_ref[:, c * FF_CHUNK:(c + 1) * FF_CHUNK],
                    preferred_element_type=jnp.float32)
        z = jnp.maximum(z, 0.0)
        acts[c] = (z * z * r_sq).astype(jnp.bfloat16)

    def pair(j):
        a = jnp.concatenate([acts.pop(2 * j), acts.pop(2 * j + 1)], axis=1)
        return a, w_down_ref[2 * j * FF_CHUNK:(2 * j + 2) * FF_CHUNK, :]

    def down(j):
        a, w = pair(j)
        part = jnp.dot(a, w, preferred_element_type=jnp.float32)
        part = jnp.where(keep, part, 0.0)
        mlp_acc[0] = part if mlp_acc[0] is None else mlp_acc[0] + part

    def down_last(j):
        a, w = pair(j)
        for r0 in range(0, T, T // 2):
            rows = slice(r0, r0 + T // 2)
            part = jnp.dot(a[rows], w, preferred_element_type=jnp.float32)
            acc = mlp_acc[0][rows] + part
            o_ref[rows, :] = h_prev[rows] + _rms(acc, g_post_mlp_ref[...])

    mlp_ops = [(up, 0), (up, 1)]
    for j in range(1, n_ff // 2):
        mlp_ops += [(down, j - 1), (up, 2 * j), (up, 2 * j + 1)]
    mlp_ops.append((down_last, n_ff // 2 - 1))

    def emit_mlp(n, last=False):
        for _ in range(n):
            if len(mlp_ops) > (0 if last else 1):
                fn, c = mlp_ops.pop(0)
                fn(c)

    emit_mlp(1)
    x = x_ref[...]
    hn = _rms(x, g_pre_ref[...]).astype(jnp.bfloat16)
    proj = jnp.dot(hn, w_in_ref[:, 0:B_OFF], preferred_element_type=jnp.float32)
    emit_mlp(1)
    cproj = jnp.dot(hn, w_in_ref[:, B_OFF:IN_COLS], preferred_element_type=jnp.float32)
    emit_mlp(1)

    lane = lax.broadcasted_iota(jnp.int32, (T, LANES), 1)
    lo_half = lane % HEAD_DIM < ROT_DIM // 2
    first_head = lane < HEAD_DIM
    cosf = cos_ref[...]
    sinf = sin_ref[...]

    scale = 1.0 / math.sqrt(HEAD_DIM)
    q_cols = []
    for c in range(ATTN_WIDTH // LANES):
        qc = proj[:, Q_OFF + c * LANES:Q_OFF + (c + 1) * LANES]
        q_cols.append((_rope(qc, cosf, sinf, lo_half) * scale).astype(jnp.bfloat16))

    k = _rope(proj[:, K_OFF:K_OFF + KV_WIDTH], cosf, sinf, lo_half)
    k_sw = pltpu.roll(k, HEAD_DIM, axis=1)
    kbuf[BLOCK:BLOCK + T, 0:LANES] = jnp.where(first_head, k, k_sw).astype(jnp.bfloat16)
    kbuf[BLOCK:BLOCK + T, LANES:2 * LANES] = jnp.where(first_head, k_sw, k).astype(jnp.bfloat16)

    v = proj[:, V_OFF:V_OFF + KV_WIDTH]
    v_sw = pltpu.roll(v, HEAD_DIM, axis=1)
    vbuf[BLOCK:BLOCK + T, 0:LANES] = jnp.where(first_head, v, v_sw).astype(jnp.bfloat16)
    vbuf[BLOCK:BLOCK + T, LANES:2 * LANES] = jnp.where(first_head, v_sw, v).astype(jnp.bfloat16)

    u = cproj[:, CONV_WIDTH:2 * CONV_WIDTH] * cproj[:, 2 * CONV_WIDTH:3 * CONV_WIDTH]
    ubuf[CONV_PAD:CONV_PAD + T, :] = u
    cw = conv_w_ref[...]
    conv = (ubuf[CONV_PAD - 2:CONV_PAD - 2 + T, :] * cw[0:1, :]
            + ubuf[CONV_PAD - 1:CONV_PAD - 1 + T, :] * cw[1:2, :]
            + u * cw[2:3, :])
    conv = cproj[:, 0:CONV_WIDTH] * conv
    ubuf[0:CONV_PAD, :] = ubuf[T:T + CONV_PAD, :]
    conv_n = _rms(conv, g_conv_ref[...]).astype(jnp.bfloat16)

    srow = lax.broadcasted_iota(jnp.int32, (GROUP_ROWS, BLOCK), 0) % BLOCK
    scol = lax.broadcasted_iota(jnp.int32, (GROUP_ROWS, BLOCK), 1)
    own_side = scol <= srow
    first_pen = jnp.where(s_idx == 0, NEG_INF, 0.0)

    grow = lax.broadcasted_iota(jnp.int32, (GROUP_ROWS, GROUP_WIDTH), 0) // BLOCK
    glane = lax.broadcasted_iota(jnp.int32, (GROUP_ROWS, GROUP_WIDTH), 1) // HEAD_DIM
    own_lanes = grow == glane
    out_head = lax.broadcasted_iota(jnp.int32, (BLOCK, GROUP_WIDTH), 1) // HEAD_DIM
    hrow = lax.broadcasted_iota(jnp.int32, (GROUP_ROWS, 1), 0) // BLOCK
    sink_cols = []
    for g in range(N_KV_HEADS):
        sc = jnp.full((GROUP_ROWS, 1), sinks_ref[g * GROUP], jnp.float32)
        for hh in range(1, GROUP):
            sc = jnp.where(hrow == hh, sinks_ref[g * GROUP + hh], sc)
        sink_cols.append(sc)

    def scores(i, g):
        r0 = i * BLOCK
        qg = jnp.concatenate(
            [q_cols[2 * g][r0:r0 + BLOCK], q_cols[2 * g + 1][r0:r0 + BLOCK]], axis=1)
        qs = jnp.concatenate([qg] * GROUP, axis=0)
        qs = jnp.where(own_lanes, qs, jnp.zeros_like(qs))
        kd = kbuf[r0:r0 + 2 * BLOCK, g * LANES:(g + 1) * LANES]
        krep = jnp.concatenate([kd, kd], axis=1)
        return lax.dot_general(qs, krep, (((1,), (1,)), ((), ())),
                               preferred_element_type=jnp.float32)

    def probs(i, g, s):
        s_prev = s[:, 0:BLOCK]
        if i == 0:
            s_prev = s_prev + first_pen
        s = jnp.where(own_side, s[:, BLOCK:2 * BLOCK], s_prev)
        sink = sink_cols[g]
        m = jnp.max(s, axis=-1, keepdims=True)
        e = jnp.exp(s - m)
        denom = jnp.sum(e, axis=-1, keepdims=True) + jnp.exp(sink - m)
        p = (e * (1.0 / denom)).astype(jnp.bfloat16)
        zeros = jnp.zeros_like(p)
        return jnp.concatenate([jnp.where(own_side, zeros, p), jnp.where(own_side, p, zeros)],
                               axis=1)

    def values(i, g, p):
        r0 = i * BLOCK
        vd = vbuf[r0:r0 + 2 * BLOCK, g * LANES:(g + 1) * LANES]
        vrep = jnp.concatenate([vd, vd], axis=1)
        pv = jnp.dot(p, vrep, preferred_element_type=jnp.float32)
        out = pv[(GROUP - 1) * BLOCK:GROUP * BLOCK]
        for hh in range(GROUP - 2, -1, -1):
            out = jnp.where(out_head == hh, pv[hh * BLOCK:(hh + 1) * BLOCK], out)
        return out

    items = [(i, g) for i in range(nblk) for g in range(N_KV_HEADS)]
    n_items = len(items)
    s_vals, p_vals, outs = {}, {}, {}
    mix_conv_cols = []
    half = D_MODEL // 2
    for step in range(n_items + 2):
        if step < n_items:
            s_vals[step] = scores(*items[step])
        else:
            c0 = (step - n_items) * half
            mix_conv_cols.append(jnp.dot(conv_n, w_out_ref[ATTN_WIDTH:, c0:c0 + half],
                                         preferred_element_type=jnp.float32))
        emit_mlp(1)
        if 0 <= step - 1 < n_items:
            p_vals[step - 1] = probs(*items[step - 1], s_vals.pop(step - 1))
        if 0 <= step - 2 < n_items:
            outs[items[step - 2]] = values(*items[step - 2], p_vals.pop(step - 2))
    attn = jnp.concatenate(
        [jnp.concatenate([outs[(i, g)] for g in range(N_KV_HEADS)], axis=1) for i in range(nblk)],
        axis=0)

    kbuf[0:BLOCK, :] = kbuf[T:T + BLOCK, :]
    vbuf[0:BLOCK, :] = vbuf[T:T + BLOCK, :]

    emit_mlp(1)
    mix_conv = jnp.where(keep, jnp.concatenate(mix_conv_cols, axis=1), 0.0)
    attn_n = _rms(attn, g_attn_ref[...]).astype(jnp.bfloat16)
    mix_out = mix_conv + jnp.dot(attn_n, w_out_ref[0:ATTN_WIDTH, :],
                                 preferred_element_type=jnp.float32)
    emit_mlp(len(mlp_ops), last=True)
    hbuf[pl.ds(w_off, T), :] = x + _rms(mix_out, g_post_ref[...])


def _rope_lane_tables(seq):
    pos = jnp.arange(seq, dtype=jnp.float32)
    inv_freq = ROPE_THETA ** (-jnp.arange(0, ROT_DIM, 2, dtype=jnp.float32) / ROT_DIM)
    ang = pos[:, None] * inv_freq[None, :]
    cos = jnp.cos(ang)
    sin = jnp.sin(ang)
    pad = HEAD_DIM - ROT_DIM
    cos_head = jnp.concatenate([cos, cos, jnp.ones((seq, pad), jnp.float32)], axis=1)
    sin_head = jnp.concatenate([-sin, sin, jnp.zeros((seq, pad), jnp.float32)], axis=1)
    reps = LANES // HEAD_DIM
    return jnp.tile(cos_head, (1, reps)), jnp.tile(sin_head, (1, reps))


def _const_spec(shape):
    return pl.BlockSpec(shape, lambda *_: (0,) * len(shape))


@jax.jit
def kernel(x, pre_mix_norm, w_in, conv_w, attn_sinks, attn_group_norm, conv_group_norm,
           w_out, post_mix_norm, pre_mlp_norm, w_up, w_down, post_mlp_norm):
    b, s, d = x.shape
    assert d == D_MODEL and s % TILE == 0
    assert pre_mix_norm.shape[0] == 1, "single-layer kernel"
    cosf, sinf = _rope_lane_tables(s)
    bf = jnp.bfloat16
    tiles_per_seq = s // TILE
    n_tiles = b * tiles_per_seq
    n_tok = b * s

    def mix_tile(t):
        return jnp.minimum(t, n_tiles - 1)

    w_up_g = (w_up[0] * pre_mlp_norm[0][:, None]).astype(bf)

    layer = pl.pallas_call(
        functools.partial(_layer_kernel, n_tiles, tiles_per_seq),
        name="layer",
        grid=(n_tiles + 1,),
        in_specs=[
            pl.BlockSpec((TILE, D_MODEL), lambda t: (mix_tile(t), 0)),
            pl.BlockSpec((TILE, LANES), lambda t: (mix_tile(t) % tiles_per_seq, 0)),
            pl.BlockSpec((TILE, LANES), lambda t: (mix_tile(t) % tiles_per_seq, 0)),
            _const_spec((1, D_MODEL)),
            _const_spec((D_MODEL, IN_COLS)),
            _const_spec((CONV_K, CONV_WIDTH)),
            pl.BlockSpec(memory_space=pltpu.SMEM),
            _const_spec((1, ATTN_WIDTH)),
            _const_spec((1, CONV_WIDTH)),
            _const_spec((D_MODEL, D_MODEL)),
            _const_spec((1, D_MODEL)),
            _const_spec((D_MODEL, D_FF)),
            _const_spec((D_FF, D_MODEL)),
            _const_spec((1, D_MODEL)),
        ],
        out_specs=pl.BlockSpec((TILE, D_MODEL), lambda t: (jnp.maximum(t - 1, 0), 0)),
        out_shape=jax.ShapeDtypeStruct((n_tok, d), x.dtype),
        scratch_shapes=[
            pltpu.VMEM((TILE + BLOCK, 2 * LANES), bf),
            pltpu.VMEM((TILE + BLOCK, 2 * LANES), bf),
            pltpu.VMEM((TILE + CONV_PAD, CONV_WIDTH), jnp.float32),
            pltpu.VMEM((2 * TILE, D_MODEL), jnp.float32),
        ],
        compiler_params=pltpu.CompilerParams(
            dimension_semantics=("arbitrary",),
            vmem_limit_bytes=VMEM_LIMIT_BYTES),
    )
    out = layer(x.reshape(n_tok, d), cosf, sinf, pre_mix_norm, w_in[0].astype(bf), conv_w[0],
                attn_sinks[0], attn_group_norm, conv_group_norm, w_out[0].astype(bf),
                post_mix_norm, w_up_g, w_down[0].astype(bf), post_mlp_norm)
    return out.reshape(b, s, d)
```

```python
import functools
import math

import jax
import jax.numpy as jnp
from jax import lax
from jax.experimental import pallas as pl
from jax.experimental.pallas import tpu as pltpu

D_MODEL = 1024
HEAD_DIM = 64
ATTN_WIDTH = 512
CONV_WIDTH = 512
N_HEADS = 8
N_KV_HEADS = 2
GROUP = N_HEADS // N_KV_HEADS
KV_WIDTH = N_KV_HEADS * HEAD_DIM
CONV_K = 3
BLOCK = 128
ROPE_THETA = 500000.0
ROT_DIM = HEAD_DIM // 4
D_FF = 4 * D_MODEL
NORM_EPS = 1e-6
NEG_INF = -1e30
IN_COLS = ATTN_WIDTH + 2 * KV_WIDTH + 3 * CONV_WIDTH

LANES = 128
GROUP_WIDTH = GROUP * HEAD_DIM
GROUP_ROWS = GROUP * BLOCK

Q_OFF = 0
K_OFF = ATTN_WIDTH
V_OFF = K_OFF + KV_WIDTH
B_OFF = V_OFF + KV_WIDTH

TILE = 512
FF_CHUNK = 512
CONV_PAD = 8
VMEM_LIMIT_BYTES = 58 * 1024 * 1024


def _rms(x, gain):
    ms = jnp.mean(x * x, axis=-1, keepdims=True)
    return x * lax.rsqrt(ms + NORM_EPS) * gain


def _rope(t, cosf, sinf, lo_half):
    fwd = pltpu.roll(t, LANES - ROT_DIM // 2, axis=1)
    bwd = pltpu.roll(t, ROT_DIM // 2, axis=1)
    return t * cosf + jnp.where(lo_half, fwd, bwd) * sinf


def _layer_kernel(n_tiles, tiles_per_seq,
                  x_ref, cos_ref, sin_ref, g_pre_ref, w_in_ref, conv_w_ref, sinks_ref,
                  g_attn_ref, g_conv_ref, w_out_ref, g_post_ref,
                  g_pre_mlp_ref, w_up_ref, w_down_ref, g_post_mlp_ref, o_ref,
                  kbuf, vbuf, ubuf, hbuf):
    T = x_ref.shape[0]
    nblk = T // BLOCK
    step_idx = pl.program_id(0)
    s_idx = lax.rem(jnp.minimum(step_idx, n_tiles - 1), tiles_per_seq)
    keep = step_idx >= 0

    @pl.when(s_idx == 0)
    def _():
        kbuf[0:BLOCK, :] = jnp.zeros((BLOCK, kbuf.shape[1]), kbuf.dtype)
        vbuf[0:BLOCK, :] = jnp.zeros((BLOCK, vbuf.shape[1]), vbuf.dtype)
        ubuf[0:CONV_PAD, :] = jnp.zeros((CONV_PAD, CONV_WIDTH), ubuf.dtype)

    @pl.when(step_idx == 0)
    def _():
        hbuf[...] = jnp.zeros(hbuf.shape, hbuf.dtype)

    w_off = pl.multiple_of(lax.rem(step_idx, 2) * T, T)
    r_off = pl.multiple_of(lax.rem(step_idx + 1, 2) * T, T)

    h_prev = hbuf[pl.ds(r_off, T), :]
    hb = (h_prev * g_pre_mlp_ref[...]).astype(jnp.bfloat16)
    r_sq = 1.0 / (jnp.mean(h_prev * h_prev, axis=-1, keepdims=True) + NORM_EPS)
    n_ff = D_FF // FF_CHUNK
    acts = {}
    mlp_acc = [None]

    def up(c):
        z = jnp.dot(hb, w_up_ref[:, c * FF_CHUNK:(c + 1) * FF_CHUNK],
                    preferred_element_type=jnp.float32)
        z = jnp.maximum(z, 0.0)
        acts[c] = (z * z * r_sq).astype(jnp.bfloat16)

    def down(c):
        part = jnp.dot(acts.pop(c), w_down_ref[c * FF_CHUNK:(c + 1) * FF_CHUNK, :],
                       preferred_element_type=jnp.float32)
        part = jnp.where(keep, part, 0.0)
        mlp_acc[0] = part if mlp_acc[0] is None else mlp_acc[0] + part

    def down_last(c):
        a = acts.pop(c)
        w = w_down_ref[c * FF_CHUNK:(c + 1) * FF_CHUNK, :]
        for r0 in range(0, T, T // 2):
            rows = slice(r0, r0 + T // 2)
            part = jnp.dot(a[rows], w, preferred_element_type=jnp.float32)
            acc = mlp_acc[0][rows] + part
            o_ref[rows, :] = h_prev[rows] + _rms(acc, g_post_mlp_ref[...])

    mlp_ops = [(up, 0)]
    for c in range(1, n_ff):
        mlp_ops += [(up, c), (down, c - 1)]
    mlp_ops.append((down_last, n_ff - 1))

    def emit_mlp(n):
        for _ in range(n):
            if mlp_ops:
                fn, c = mlp_ops.pop(0)
                fn(c)

    emit_mlp(1)
    x = x_ref[...]
    hn = _rms(x, g_pre_ref[...]).astype(jnp.bfloat16)
    proj = jnp.dot(hn, w_in_ref[:, 0:B_OFF], preferred_element_type=jnp.float32)
    emit_mlp(1)
    cproj = jnp.dot(hn, w_in_ref[:, B_OFF:IN_COLS], preferred_element_type=jnp.float32)
    emit_mlp(1)

    lane = lax.broadcasted_iota(jnp.int32, (T, LANES), 1)
    lo_half = lane % HEAD_DIM < ROT_DIM // 2
    first_head = lane < HEAD_DIM
    cosf = cos_ref[...]
    sinf = sin_ref[...]

    scale = 1.0 / math.sqrt(HEAD_DIM)
    q_cols = []
    for c in range(ATTN_WIDTH // LANES):
        qc = proj[:, Q_OFF + c * LANES:Q_OFF + (c + 1) * LANES]
        q_cols.append((_rope(qc, cosf, sinf, lo_half) * scale).astype(jnp.bfloat16))

    k = _rope(proj[:, K_OFF:K_OFF + KV_WIDTH], cosf, sinf, lo_half)
    k_sw = pltpu.roll(k, HEAD_DIM, axis=1)
    kbuf[BLOCK:BLOCK + T, 0:LANES] = jnp.where(first_head, k, k_sw).astype(jnp.bfloat16)
    kbuf[BLOCK:BLOCK + T, LANES:2 * LANES] = jnp.where(first_head, k_sw, k).astype(jnp.bfloat16)

    v = proj[:, V_OFF:V_OFF + KV_WIDTH]
    v_sw = pltpu.roll(v, HEAD_DIM, axis=1)
    vbuf[BLOCK:BLOCK + T, 0:LANES] = jnp.where(first_head, v, v_sw).astype(jnp.bfloat16)
    vbuf[BLOCK:BLOCK + T, LANES:2 * LANES] = jnp.where(first_head, v_sw, v).astype(jnp.bfloat16)

    u = cproj[:, CONV_WIDTH:2 * CONV_WIDTH] * cproj[:, 2 * CONV_WIDTH:3 * CONV_WIDTH]
    ubuf[CONV_PAD:CONV_PAD + T, :] = u
    cw = conv_w_ref[...]
    conv = (ubuf[CONV_PAD - 2:CONV_PAD - 2 + T, :] * cw[0:1, :]
            + ubuf[CONV_PAD - 1:CONV_PAD - 1 + T, :] * cw[1:2, :]
            + u * cw[2:3, :])
    conv = cproj[:, 0:CONV_WIDTH] * conv
    ubuf[0:CONV_PAD, :] = ubuf[T:T + CONV_PAD, :]
    conv_n = _rms(conv, g_conv_ref[...]).astype(jnp.bfloat16)

    srow = lax.broadcasted_iota(jnp.int32, (GROUP_ROWS, BLOCK), 0) % BLOCK
    scol = lax.broadcasted_iota(jnp.int32, (GROUP_ROWS, BLOCK), 1)
    own_side = scol <= srow
    first_pen = jnp.where(s_idx == 0, NEG_INF, 0.0)

    grow = lax.broadcasted_iota(jnp.int32, (GROUP_ROWS, GROUP_WIDTH), 0) // BLOCK
    glane = lax.broadcasted_iota(jnp.int32, (GROUP_ROWS, GROUP_WIDTH), 1) // HEAD_DIM
    own_lanes = grow == glane
    out_head = lax.broadcasted_iota(jnp.int32, (BLOCK, GROUP_WIDTH), 1) // HEAD_DIM
    hrow = lax.broadcasted_iota(jnp.int32, (GROUP_ROWS, 1), 0) // BLOCK
    sink_cols = []
    for g in range(N_KV_HEADS):
        sc = jnp.full((GROUP_ROWS, 1), sinks_ref[g * GROUP], jnp.float32)
        for hh in range(1, GROUP):
            sc = jnp.where(hrow == hh, sinks_ref[g * GROUP + hh], sc)
        sink_cols.append(sc)

    def scores(i, g):
        r0 = i * BLOCK
        qg = jnp.concatenate(
            [q_cols[2 * g][r0:r0 + BLOCK], q_cols[2 * g + 1][r0:r0 + BLOCK]], axis=1)
        qs = jnp.concatenate([qg] * GROUP, axis=0)
        qs = jnp.where(own_lanes, qs, jnp.zeros_like(qs))
        kd = kbuf[r0:r0 + 2 * BLOCK, g * LANES:(g + 1) * LANES]
        krep = jnp.concatenate([kd, kd], axis=1)
        return lax.dot_general(qs, krep, (((1,), (1,)), ((), ())),
                               preferred_element_type=jnp.float32)

    def probs(i, g, s):
        s_prev = s[:, 0:BLOCK]
        if i == 0:
            s_prev = s_prev + first_pen
        s = jnp.where(own_side, s[:, BLOCK:2 * BLOCK], s_prev)
        sink = sink_cols[g]
        m = jnp.max(s, axis=-1, keepdims=True)
        e = jnp.exp(s - m)
        denom = jnp.sum(e, axis=-1, keepdims=True) + jnp.exp(sink - m)
        p = (e * (1.0 / denom)).astype(jnp.bfloat16)
        zeros = jnp.zeros_like(p)
        return jnp.concatenate([jnp.where(own_side, zeros, p), jnp.where(own_side, p, zeros)],
                               axis=1)

    def values(i, g, p):
        r0 = i * BLOCK
        vd = vbuf[r0:r0 + 2 * BLOCK, g * LANES:(g + 1) * LANES]
        vrep = jnp.concatenate([vd, vd], axis=1)
        pv = jnp.dot(p, vrep, preferred_element_type=jnp.float32)
        out = pv[(GROUP - 1) * BLOCK:GROUP * BLOCK]
        for hh in range(GROUP - 2, -1, -1):
            out = jnp.where(out_head == hh, pv[hh * BLOCK:(hh + 1) * BLOCK], out)
        return out

    items = [(i, g) for i in range(nblk) for g in range(N_KV_HEADS)]
    n_items = len(items)
    s_vals, p_vals, outs = {}, {}, {}
    mix_conv_cols = []
    half = D_MODEL // 2
    for step in range(n_items + 2):
        if step < n_items:
            s_vals[step] = scores(*items[step])
        else:
            c0 = (step - n_items) * half
            mix_conv_cols.append(jnp.dot(conv_n, w_out_ref[ATTN_WIDTH:, c0:c0 + half],
                                         preferred_element_type=jnp.float32))
        emit_mlp(1)
        if 0 <= step - 1 < n_items:
            p_vals[step - 1] = probs(*items[step - 1], s_vals.pop(step - 1))
        if 0 <= step - 2 < n_items:
            outs[items[step - 2]] = values(*items[step - 2], p_vals.pop(step - 2))
    attn = jnp.concatenate(
        [jnp.concatenate([outs[(i, g)] for g in range(N_KV_HEADS)], axis=1) for i in range(nblk)],
        axis=0)

    kbuf[0:BLOCK, :] = kbuf[T:T + BLOCK, :]
    vbuf[0:BLOCK, :] = vbuf[T:T + BLOCK, :]

    emit_mlp(1)
    mix_conv = jnp.where(keep, jnp.concatenate(mix_conv_cols, axis=1), 0.0)
    attn_n = _rms(attn, g_attn_ref[...]).astype(jnp.bfloat16)
    mix_out = mix_conv + jnp.dot(attn_n, w_out_ref[0:ATTN_WIDTH, :],
                                 preferred_element_type=jnp.float32)
    emit_mlp(len(mlp_ops))
    hbuf[pl.ds(w_off, T), :] = x + _rms(mix_out, g_post_ref[...])


def _rope_lane_tables(seq):
    pos = jnp.arange(seq, dtype=jnp.float32)
    inv_freq = ROPE_THETA ** (-jnp.arange(0, ROT_DIM, 2, dtype=jnp.float32) / ROT_DIM)
    ang = pos[:, None] * inv_freq[None, :]
    cos = jnp.cos(ang)
    sin = jnp.sin(ang)
    pad = HEAD_DIM - ROT_DIM
    cos_head = jnp.concatenate([cos, cos, jnp.ones((seq, pad), jnp.float32)], axis=1)
    sin_head = jnp.concatenate([-sin, sin, jnp.zeros((seq, pad), jnp.float32)], axis=1)
    reps = LANES // HEAD_DIM
    return jnp.tile(cos_head, (1, reps)), jnp.tile(sin_head, (1, reps))


def _const_spec(shape):
    return pl.BlockSpec(shape, lambda *_: (0,) * len(shape))


@jax.jit
def kernel(x, pre_mix_norm, w_in, conv_w, attn_sinks, attn_group_norm, conv_group_norm,
           w_out, post_mix_norm, pre_mlp_norm, w_up, w_down, post_mlp_norm):
    b, s, d = x.shape
    assert d == D_MODEL and s % TILE == 0
    assert pre_mix_norm.shape[0] == 1, "single-layer kernel"
    cosf, sinf = _rope_lane_tables(s)
    bf = jnp.bfloat16
    tiles_per_seq = s // TILE
    n_tiles = b * tiles_per_seq
    n_tok = b * s

    def mix_tile(t):
        return jnp.minimum(t, n_tiles - 1)

    layer = pl.pallas_call(
        functools.partial(_layer_kernel, n_tiles, tiles_per_seq),
        name="layer",
        grid=(n_tiles + 1,),
        in_specs=[
            pl.BlockSpec((TILE, D_MODEL), lambda t: (mix_tile(t), 0)),
            pl.BlockSpec((TILE, LANES), lambda t: (mix_tile(t) % tiles_per_seq, 0)),
            pl.BlockSpec((TILE, LANES), lambda t: (mix_tile(t) % tiles_per_seq, 0)),
            _const_spec((1, D_MODEL)),
            _const_spec((D_MODEL, IN_COLS)),
            _const_spec((CONV_K, CONV_WIDTH)),
            pl.BlockSpec(memory_space=pltpu.SMEM),
            _const_spec((1, ATTN_WIDTH)),
            _const_spec((1, CONV_WIDTH)),
            _const_spec((D_MODEL, D_MODEL)),
            _const_spec((1, D_MODEL)),
            _const_spec((1, D_MODEL)),
            _const_spec((D_MODEL, D_FF)),
            _const_spec((D_FF, D_MODEL)),
            _const_spec((1, D_MODEL)),
        ],
        out_specs=pl.BlockSpec((TILE, D_MODEL), lambda t: (jnp.maximum(t - 1, 0), 0)),
        out_shape=jax.ShapeDtypeStruct((n_tok, d), x.dtype),
        scratch_shapes=[
            pltpu.VMEM((TILE + BLOCK, 2 * LANES), bf),
            pltpu.VMEM((TILE + BLOCK, 2 * LANES), bf),
            pltpu.VMEM((TILE + CONV_PAD, CONV_WIDTH), jnp.float32),
            pltpu.VMEM((2 * TILE, D_MODEL), jnp.float32),
        ],
        compiler_params=pltpu.CompilerParams(
            dimension_semantics=("arbitrary",),
            vmem_limit_bytes=VMEM_LIMIT_BYTES),
    )
    out = layer(x.reshape(n_tok, d), cosf, sinf, pre_mix_norm, w_in[0].astype(bf), conv_w[0],
                attn_sinks[0], attn_group_norm, conv_group_norm, w_out[0].astype(bf),
                post_mix_norm, pre_mlp_norm, w_up[0].astype(bf), w_down[0].astype(bf),
                post_mlp_norm)
    return out.reshape(b, s, d)
```
